```python
import jax, jax.numpy as jnp
from jax import lax
import numpy as np

D_MODEL = 1024
BATCH = 8
SEQ = 2048
DEPTH = 4
DEC_BATCH = 128
DEC_SEQ = 8
PAST_LEN = 8192
PAGE_SIZE = 128

N_MIXERS = 2
N_CONV_LAYERS = (DEPTH + 1) // 2
N_MLA_LAYERS = DEPTH // 2
CONV_W = 31
MLA_HEADS = 16
Q_LORA = 384
KV_LORA = 256
QK_NOPE = 64
QK_ROPE = 32
V_HEAD = 64
MLA_SCALE = (QK_NOPE + QK_ROPE) ** -0.5
ROPE_THETA = 10000.0
Q_BLOCK = 128
N_MEM = 256
XA_HEADS = 4
XA_HEAD_DIM = 128
XA_WIDTH = XA_HEADS * XA_HEAD_DIM
D_FF = 2816
FFN_RESID = 0.5
RMS_EPS = 1e-6
LN_EPS = 1e-5
NEG_INF = -1e30

kernel_name = "hybrid_conformer_mla_decoder_step"


def rms_norm(x, g):
    xf = x.astype(jnp.float32)
    y = xf * lax.rsqrt(jnp.mean(xf * xf, axis=-1, keepdims=True) + RMS_EPS)
    return (y * g.astype(jnp.float32)).astype(x.dtype)


def layer_norm(x, g, b):
    xf = x.astype(jnp.float32)
    mu = jnp.mean(xf, axis=-1, keepdims=True)
    var = jnp.mean(jnp.square(xf - mu), axis=-1, keepdims=True)
    y = (xf - mu) * lax.rsqrt(var + LN_EPS) * g.astype(jnp.float32) + b.astype(jnp.float32)
    return y.astype(x.dtype)


def swiglu_ffn(x, w_in, w_out):
    gate, up = jnp.split(x @ w_in, 2, axis=-1)
    return (jax.nn.silu(gate) * up) @ w_out


def rope_angles(pos):
    inv_freq = ROPE_THETA ** (-jnp.arange(0, QK_ROPE, 2, dtype=jnp.float32) / QK_ROPE)
    ang = pos.astype(jnp.float32)[:, None] * inv_freq[None, :]
    return jnp.cos(ang), jnp.sin(ang)


def apply_rope(x, cos, sin):
    xf = x.astype(jnp.float32)
    x1, x2 = jnp.split(xf, 2, axis=-1)
    out = jnp.concatenate([x1 * cos - x2 * sin, x2 * cos + x1 * sin], axis=-1)
    return out.astype(x.dtype)


def conv_module(x, prev, w_pw1, b_pw1, w_dw, b_dw, ln_g, ln_b, w_pw2, b_pw2):
    a, gate = jnp.split(x @ w_pw1 + b_pw1, 2, axis=-1)
    u = a * jax.nn.sigmoid(gate)
    ext = jnp.concatenate([prev, u], axis=1)
    y = lax.conv_general_dilated(
        ext, w_dw[:, None, :], window_strides=(1,), padding="VALID",
        dimension_numbers=("NWC", "WIO", "NWC"), feature_group_count=ext.shape[-1])
    y = y + b_dw
    y = jax.nn.silu(layer_norm(y, ln_g, ln_b))
    return y @ w_pw2 + b_pw2, ext[:, -(CONV_W - 1):]


def latent_attention(q_lat, q_pe, k_lat, k_pe, q_pos):
    b, q, h, l = q_lat.shape
    r = q_pe.shape[-1]
    k_pos = jnp.arange(k_lat.shape[1], dtype=jnp.int32)
    blk = Q_BLOCK if q % Q_BLOCK == 0 else q
    nb = q // blk

    def one_block(args):
        ql, qp, pos = args
        s = (jnp.einsum("bqhl,bkl->bhqk", ql, k_lat) +
             jnp.einsum("bqhr,bkr->bhqk", qp, k_pe)).astype(jnp.float32) * MLA_SCALE
        mask = k_pos[None, :] <= pos[:, None]
        s = jnp.where(mask[None, None], s, NEG_INF)
        p = jax.nn.softmax(s, axis=-1).astype(k_lat.dtype)
        return jnp.einsum("bhqk,bkl->bqhl", p, k_lat)

    qlb = jnp.moveaxis(q_lat.reshape(b, nb, blk, h, l), 1, 0)
    qpb = jnp.moveaxis(q_pe.reshape(b, nb, blk, h, r), 1, 0)
    out = lax.map(one_block, (qlb, qpb, q_pos.reshape(nb, blk)))
    return jnp.moveaxis(out, 0, 1).reshape(b, q, h, l)


def mla(x, pos, past, w_in, q_g, kv_g, w_uq, w_ukv, w_o):
    b, s, _ = x.shape
    down = x @ w_in
    c_q = rms_norm(down[..., :Q_LORA], q_g)
    c_kv = rms_norm(down[..., Q_LORA:Q_LORA + KV_LORA], kv_g)
    k_pe = down[..., Q_LORA + KV_LORA:]
    q = jnp.einsum("bsc,chd->bshd", c_q, w_uq)
    q_nope, q_pe = q[..., :QK_NOPE], q[..., QK_NOPE:]
    cos, sin = rope_angles(pos)
    q_pe = apply_rope(q_pe, cos[:, None, :], sin[:, None, :])
    k_pe = apply_rope(k_pe, cos, sin)
    w_uk, w_uv = w_ukv[..., :QK_NOPE], w_ukv[..., QK_NOPE:]
    q_lat = jnp.einsum("bshn,lhn->bshl", q_nope, w_uk)
    if past is None:
        keys_lat, keys_pe = c_kv, k_pe
    else:
        keys_lat = jnp.concatenate([past[0], c_kv], axis=1)
        keys_pe = jnp.concatenate([past[1], k_pe], axis=1)
    o_lat = latent_attention(q_lat, q_pe, keys_lat, keys_pe, pos)
    o = jnp.einsum("bshl,lhv->bshv", o_lat, w_uv).reshape(b, s, MLA_HEADS * V_HEAD)
    return o @ w_o, c_kv, k_pe


def memory_kv(mem, g, w_kv):
    b, m, _ = mem.shape
    k, v = jnp.split(rms_norm(mem, g) @ w_kv, 2, axis=-1)
    return (k.reshape(b, m, XA_HEADS, XA_HEAD_DIM), v.reshape(b, m, XA_HEADS, XA_HEAD_DIM))


def memory_attend(x, k, v, w_q, w_o):
    b, s, _ = x.shape
    q = (x @ w_q).reshape(b, s, XA_HEADS, XA_HEAD_DIM)
    sc = jnp.einsum("bshd,bmhd->bhsm", q, k).astype(jnp.float32) * (XA_HEAD_DIM ** -0.5)
    p = jax.nn.softmax(sc, axis=-1).astype(v.dtype)
    o = jnp.einsum("bhsm,bmhd->bshd", p, v).reshape(b, s, XA_WIDTH)
    return o @ w_o


def trunk(x, pos, conv_prevs, mla_pasts, mem_k, mem_v, P):
    h = x
    conv_new, lat_new, rope_new = [], [], []
    for i in range(DEPTH):
        g = P["norm_gain"][i]
        f = swiglu_ffn(rms_norm(h, g[0]), P["ffn_w_in"][i, 0], P["ffn_w_out"][i, 0])
        h = h + FFN_RESID * rms_norm(f, g[1])
        t = rms_norm(h, g[2])
        j = i // N_MIXERS
        if i % N_MIXERS == 0:
            m, st = conv_module(t, conv_prevs[j], P["conv_w_pw1"][j], P["conv_b_pw1"][j],
                                P["conv_w_dw"][j], P["conv_b_dw"][j], P["conv_ln_g"][j],
                                P["conv_ln_b"][j], P["conv_w_pw2"][j], P["conv_b_pw2"][j])
            conv_new.append(st)
        else:
            m, lat, kr = mla(t, pos, mla_pasts[j], P["mla_w_in"][j], P["mla_q_norm"][j],
                             P["mla_kv_norm"][j], P["mla_w_uq"][j], P["mla_w_ukv"][j], P["mla_w_o"][j])
            lat_new.append(lat)
            rope_new.append(kr)
        h = h + rms_norm(m, g[3])
        c = memory_attend(rms_norm(h, g[4]), mem_k[i], mem_v[i], P["xa_w_q"][i], P["xa_w_o"][i])
        h = h + rms_norm(c, g[5])
        f = swiglu_ffn(rms_norm(h, g[6]), P["ffn_w_in"][i, 1], P["ffn_w_out"][i, 1])
        h = h + FFN_RESID * rms_norm(f, g[7])
    return h, conv_new, lat_new, rope_new


def setup_inputs(seed: int = 0) -> dict:
    key = jax.random.key(seed)
    ks = iter(jax.random.split(key, 48))
    f32 = jnp.float32
    D = D_MODEL

    def nrm(shape, scale=1.0):
        return jax.random.normal(next(ks), shape, f32) * scale

    n_pages = PAST_LEN // PAGE_SIZE
    n_used = DEC_BATCH * n_pages
    n_pool = (5 * n_used + 3) // 4
    page_table = jax.random.permutation(next(ks), n_pool)[:n_used].reshape(
        DEC_BATCH, n_pages).astype(jnp.int32)
    return {
        "x_prompt": nrm((BATCH, SEQ, D)),
        "x_sample": nrm((DEC_BATCH, DEC_SEQ, D)),
        "state_conv_l0": nrm((DEC_BATCH, CONV_W - 1, D), 0.5),
        "state_conv_l2": nrm((DEC_BATCH, CONV_W - 1, D), 0.5),
        "cache_mla_latent_l1": nrm((n_pool, PAGE_SIZE, KV_LORA)),
        "cache_mla_krope_l1": nrm((n_pool, PAGE_SIZE, QK_ROPE)),
        "cache_mla_latent_l3": nrm((n_pool, PAGE_SIZE, KV_LORA)),
        "cache_mla_krope_l3": nrm((n_pool, PAGE_SIZE, QK_ROPE)),
        "cache_mem_k": nrm((DEPTH, DEC_BATCH, N_MEM, XA_HEADS, XA_HEAD_DIM)),
        "cache_mem_v": nrm((DEPTH, DEC_BATCH, N_MEM, XA_HEADS, XA_HEAD_DIM)),
        "page_table": page_table,
        "mem_prompt": nrm((BATCH, N_MEM, D)),
        "norm_gain": 1.0 + nrm((DEPTH, 8, D), 0.05),
        "ffn_w_in": nrm((DEPTH, 2, D, 2 * D_FF), D ** -0.5),
        "ffn_w_out": nrm((DEPTH, 2, D_FF, D), D_FF ** -0.5),
        "conv_w_pw1": nrm((N_CONV_LAYERS, D, 2 * D), D ** -0.5),
        "conv_b_pw1": nrm((N_CONV_LAYERS, 2 * D), 0.02),
        "conv_w_dw": nrm((N_CONV_LAYERS, CONV_W, D), CONV_W ** -0.5),
        "conv_b_dw": nrm((N_CONV_LAYERS, D), 0.02),
        "conv_ln_g": 1.0 + nrm((N_CONV_LAYERS, D), 0.05),
        "conv_ln_b": nrm((N_CONV_LAYERS, D), 0.02),
        "conv_w_pw2": nrm((N_CONV_LAYERS, D, D), D ** -0.5),
        "conv_b_pw2": nrm((N_CONV_LAYERS, D), 0.02),
        "mla_w_in": nrm((N_MLA_LAYERS, D, Q_LORA + KV_LORA + QK_ROPE), D ** -0.5),
        "mla_q_norm": 1.0 + nrm((N_MLA_LAYERS, Q_LORA), 0.05),
        "mla_kv_norm": 1.0 + nrm((N_MLA_LAYERS, KV_LORA), 0.05),
        "mla_w_uq": nrm((N_MLA_LAYERS, Q_LORA, MLA_HEADS, QK_NOPE + QK_ROPE), Q_LORA ** -0.5),
        "mla_w_ukv": nrm((N_MLA_LAYERS, KV_LORA, MLA_HEADS, QK_NOPE + V_HEAD), KV_LORA ** -0.5),
        "mla_w_o": nrm((N_MLA_LAYERS, MLA_HEADS * V_HEAD, D), (MLA_HEADS * V_HEAD) ** -0.5),
        "xa_mem_norm": 1.0 + nrm((DEPTH, D), 0.05),
        "xa_w_q": nrm((DEPTH, D, XA_WIDTH), D ** -0.5),
        "xa_w_kv": nrm((DEPTH, D, 2 * XA_WIDTH), D ** -0.5),
        "xa_w_o": nrm((DEPTH, XA_WIDTH, D), XA_WIDTH ** -0.5),
    }


def reference(x_prompt, x_sample, state_conv_l0, state_conv_l2,
              cache_mla_latent_l1, cache_mla_krope_l1, cache_mla_latent_l3, cache_mla_krope_l3,
              cache_mem_k, cache_mem_v, page_table, mem_prompt,
              norm_gain, ffn_w_in, ffn_w_out,
              conv_w_pw1, conv_b_pw1, conv_w_dw, conv_b_dw, conv_ln_g, conv_ln_b, conv_w_pw2, conv_b_pw2,
              mla_w_in, mla_q_norm, mla_kv_norm, mla_w_uq, mla_w_ukv, mla_w_o,
              xa_mem_norm, xa_w_q, xa_w_kv, xa_w_o):
    P = dict(norm_gain=norm_gain, ffn_w_in=ffn_w_in, ffn_w_out=ffn_w_out,
             conv_w_pw1=conv_w_pw1, conv_b_pw1=conv_b_pw1, conv_w_dw=conv_w_dw, conv_b_dw=conv_b_dw,
             conv_ln_g=conv_ln_g, conv_ln_b=conv_ln_b, conv_w_pw2=conv_w_pw2, conv_b_pw2=conv_b_pw2,
             mla_w_in=mla_w_in, mla_q_norm=mla_q_norm, mla_kv_norm=mla_kv_norm,
             mla_w_uq=mla_w_uq, mla_w_ukv=mla_w_ukv, mla_w_o=mla_w_o,
             xa_w_q=xa_w_q, xa_w_o=xa_w_o)

    b_p, s_p, d = x_prompt.shape
    pos_p = jnp.arange(s_p, dtype=jnp.int32)
    mem_kv_p = [memory_kv(mem_prompt, xa_mem_norm[i], xa_w_kv[i]) for i in range(DEPTH)]
    mem_k_p = [kv[0] for kv in mem_kv_p]
    mem_v_p = [kv[1] for kv in mem_kv_p]
    conv_prev_p = [jnp.zeros((b_p, CONV_W - 1, d), x_prompt.dtype) for _ in range(N_CONV_LAYERS)]
    mla_past_p = [None for _ in range(N_MLA_LAYERS)]
    y_prompt, p_conv, p_lat, p_rope = trunk(x_prompt, pos_p, conv_prev_p, mla_past_p,
                                            mem_k_p, mem_v_p, P)
    mem_k_prompt = jnp.stack(mem_k_p, axis=0)
    mem_v_prompt = jnp.stack(mem_v_p, axis=0)

    n_seq, n_pages = page_table.shape

    def gather(pool):
        return pool[page_table].reshape(n_seq, n_pages * pool.shape[1], pool.shape[2])

    past_len = n_pages * cache_mla_latent_l1.shape[1]
    pos_s = past_len + jnp.arange(x_sample.shape[1], dtype=jnp.int32)
    mla_past_s = [(gather(cache_mla_latent_l1), gather(cache_mla_krope_l1)),
                  (gather(cache_mla_latent_l3), gather(cache_mla_krope_l3))]
    conv_prev_s = [state_conv_l0, state_conv_l2]
    mem_k_s = [cache_mem_k[i] for i in range(DEPTH)]
    mem_v_s = [cache_mem_v[i] for i in range(DEPTH)]
    y_sample, s_conv, s_lat, s_rope = trunk(x_sample, pos_s, conv_prev_s, mla_past_s,
                                            mem_k_s, mem_v_s, P)

    return (y_prompt, y_sample,
            p_conv[0], p_conv[1], p_lat[0], p_rope[0], p_lat[1], p_rope[1],
            mem_k_prompt, mem_v_prompt,
            s_conv[0], s_conv[1], s_lat[0], s_rope[0], s_lat[1], s_rope[1])
```

```python
import functools

import jax
import jax.numpy as jnp
from jax import lax
from jax.experimental import pallas as pl
from jax.experimental.pallas import tpu as pltpu

F32 = jnp.float32
BF16 = jnp.bfloat16

RMS_EPS = 1e-6
LN_EPS = 1e-5
ROPE_THETA = 10000.0
FFN_RESID = 0.5
NEG_INF = -1e30

Q_LORA = 384
KV_LORA = 256
QK_NOPE = 64
QK_ROPE = 32
XA_HEAD_DIM = 128

LANES = 128
SUBLANES = 8
VMEM_LIMIT_BYTES = 52 * 1024 * 1024

ROW_TILE = 512
CONV_HALO = 32
PAGES_PER_STEP = 8
XATTN_GROUP = 8
CONV_GROUP = 8


def _cparams(*sem):
    return pltpu.CompilerParams(dimension_semantics=sem, vmem_limit_bytes=VMEM_LIMIT_BYTES)


def _row_tile(t):
    return ROW_TILE if t % ROW_TILE == 0 else t


def _rms(x, g):
    return x * lax.rsqrt(jnp.mean(x * x, axis=-1, keepdims=True) + RMS_EPS) * g


def _dot(a, b):
    return jnp.dot(a, b, preferred_element_type=F32)


def _dot_nt(a, b):
    return lax.dot_general(a, b, (((1,), (1,)), ((), ())), preferred_element_type=F32)


def _gain_spec(idx, d, ngrid):
    if ngrid == 1:
        return pl.BlockSpec((None, 1, d), lambda i: (idx, 0, 0))
    return pl.BlockSpec((None, 1, d), lambda i, j: (idx, 0, 0))


def _ffn_kernel(h_ref, gpre_ref, gpost_ref, wg_ref, wu_ref, wo_ref, out_ref, xn_ref, acc_ref):
    j = pl.program_id(1)

    @pl.when(j == 0)
    def _():
        xn_ref[...] = _rms(h_ref[...], gpre_ref[...]).astype(BF16)
        acc_ref[...] = jnp.zeros_like(acc_ref)

    xn = xn_ref[...]
    gate = _dot(xn, wg_ref[...])
    up = _dot(xn, wu_ref[...])
    a = (gate * jax.nn.sigmoid(gate) * up).astype(BF16)
    acc_ref[...] += _dot(a, wo_ref[...])

    @pl.when(j == pl.num_programs(1) - 1)
    def _():
        out_ref[...] = h_ref[...] + FFN_RESID * _rms(acc_ref[...], gpost_ref[...])


def _ffn(h, gains, gi_pre, gi_post, w_in, w_out, li):
    t, d = h.shape
    f = w_out.shape[1]
    tm = _row_tile(t)
    tf = f // 2 if (f // 2) % LANES == 0 else f
    nf = f // tf
    return pl.pallas_call(
        _ffn_kernel,
        grid=(t // tm, nf),
        in_specs=[
            pl.BlockSpec((tm, d), lambda i, j: (i, 0)),
            _gain_spec(gi_pre, d, 2),
            _gain_spec(gi_post, d, 2),
            pl.BlockSpec((None, d, tf), lambda i, j: (li, 0, j)),
            pl.BlockSpec((None, d, tf), lambda i, j: (li, 0, j + nf)),
            pl.BlockSpec((None, tf, d), lambda i, j: (li, j, 0)),
        ],
        out_specs=pl.BlockSpec((tm, d), lambda i, j: (i, 0)),
        out_shape=jax.ShapeDtypeStruct((t, d), F32),
        scratch_shapes=[pltpu.VMEM((tm, d), BF16), pltpu.VMEM((tm, d), F32)],
        compiler_params=_cparams("parallel", "arbitrary"),
    )(h, gains, gains, w_in, w_in, w_out)


def _memkv_kernel(mem_ref, g_ref, w_ref, k_ref, v_ref):
    xn = _rms(mem_ref[...], g_ref[...]).astype(BF16)
    kv = _dot(xn, w_ref[...])
    xw = k_ref.shape[-1]
    k_ref[...] = kv[:, :xw]
    v_ref[...] = kv[:, xw:]


def _memkv(mem, gains, w_kv):
    m, d = mem.shape
    depth, _, xw2 = w_kv.shape
    xw = xw2 // 2
    tm = _row_tile(m)
    out = jax.ShapeDtypeStruct((depth, m, xw), F32)
    return pl.pallas_call(
        _memkv_kernel,
        grid=(depth, m // tm),
        in_specs=[
            pl.BlockSpec((tm, d), lambda l, i: (i, 0)),
            pl.BlockSpec((None, 1, d), lambda l, i: (l, 0, 0)),
            pl.BlockSpec((None, d, xw2), lambda l, i: (l, 0, 0)),
        ],
        out_specs=[pl.BlockSpec((None, tm, xw), lambda l, i: (l, i, 0))] * 2,
        out_shape=[out, out],
        compiler_params=_cparams("parallel", "parallel"),
    )(mem, gains, w_kv)


def _xattn_kernel(h_ref, k_ref, v_ref, gpre_ref, gpost_ref, wq_ref, wo_ref, out_ref, *, heads):
    g, ts, d = h_ref.shape
    xw = wq_ref.shape[-1]
    hd = xw // heads
    h = h_ref[...].reshape(g * ts, d)
    xn = _rms(h, gpre_ref[...]).astype(BF16)
    q = (_dot(xn, wq_ref[...]) * (hd ** -0.5)).reshape(g, ts, xw).astype(BF16)
    k = k_ref[...].astype(BF16)
    v = v_ref[...].astype(BF16)
    outs = []
    for hh in range(heads):
        sl = slice(hh * hd, (hh + 1) * hd)
        s = jnp.einsum("gqd,gkd->gqk", q[:, :, sl], k[:, :, sl], preferred_element_type=F32)
        e = jnp.exp(s - jnp.max(s, axis=-1, keepdims=True))
        l = jnp.sum(e, axis=-1, keepdims=True)
        o = jnp.einsum("gqk,gkd->gqd", e.astype(BF16), v[:, :, sl], preferred_element_type=F32)
        outs.append(o / l)
    o = jnp.concatenate(outs, axis=-1).reshape(g * ts, xw).astype(BF16)
    c = _dot(o, wo_ref[...])
    out_ref[...] = (h + _rms(c, gpost_ref[...])).reshape(g, ts, d)


def _xattn(h, k, v, kv_base, gains, gi_pre, gi_post, w_q, w_o, li, group, ts):
    b, s, d = h.shape
    _, m, xw = k.shape
    heads = xw // XA_HEAD_DIM
    kvb = kv_base // group
    return pl.pallas_call(
        functools.partial(_xattn_kernel, heads=heads),
        grid=(b // group, s // ts),
        in_specs=[
            pl.BlockSpec((group, ts, d), lambda i, j: (i, j, 0)),
            pl.BlockSpec((group, m, xw), lambda i, j: (kvb + i, 0, 0)),
            pl.BlockSpec((group, m, xw), lambda i, j: (kvb + i, 0, 0)),
            _gain_spec(gi_pre, d, 2),
            _gain_spec(gi_post, d, 2),
            pl.BlockSpec((None, d, xw), lambda i, j: (li, 0, 0)),
            pl.BlockSpec((None, xw, d), lambda i, j: (li, 0, 0)),
        ],
        out_specs=pl.BlockSpec((group, ts, d), lambda i, j: (i, j, 0)),
        out_shape=jax.ShapeDtypeStruct((b, s, d), F32),
        compiler_params=_cparams("parallel", "parallel"),
    )(h, k, v, gains, gains, w_q, w_o)


def _conv_a_kernel(h_ref, g_ref, w_ref, b_ref, u_ref):
    d = u_ref.shape[-1]
    xn = _rms(h_ref[...], g_ref[...]).astype(BF16)
    y = _dot(xn, w_ref[...]) + b_ref[...]
    u_ref[...] = y[:, :d] * jax.nn.sigmoid(y[:, d:])


def _conv_a(h, gains, gi, w_pw1, b_pw1, lj):
    t, d = h.shape
    tm = _row_tile(t)
    return pl.pallas_call(
        _conv_a_kernel,
        grid=(t // tm,),
        in_specs=[
            pl.BlockSpec((tm, d), lambda i: (i, 0)),
            _gain_spec(gi, d, 1),
            pl.BlockSpec((None, d, 2 * d), lambda i: (lj, 0, 0)),
            pl.BlockSpec((None, 1, 2 * d), lambda i: (lj, 0, 0)),
        ],
        out_specs=pl.BlockSpec((tm, d), lambda i: (i, 0)),
        out_shape=jax.ShapeDtypeStruct((t, d), F32),
        compiler_params=_cparams("parallel"),
    )(h, gains, w_pw1, b_pw1)


def _conv_b_kernel(h_ref, u_ref, prev_ref, wdw_ref, bdw_ref, lng_ref, lnb_ref, w2_ref, b2_ref,
                   gpost_ref, out_ref, e_ref, y_ref, *, width, from_prev_tile):
    g, ts, d = u_ref.shape
    lead = CONV_HALO - (width - 1)
    ext_rows = CONV_HALO + ts

    if from_prev_tile:
        first = pl.program_id(1) == 0

        @pl.when(first)
        def _():
            e_ref[0, :, :CONV_HALO, :] = jnp.zeros((g, CONV_HALO, d), F32)

        @pl.when(jnp.logical_not(first))
        def _():
            e_ref[0, :, :CONV_HALO, :] = prev_ref[...]
    else:
        e_ref[0, :, :lead, :] = jnp.zeros((g, lead, d), F32)
        e_ref[0, :, lead:CONV_HALO, :] = prev_ref[...]
    e_ref[0, :, CONV_HALO:, :] = u_ref[...]

    span = ext_rows - SUBLANES
    for r in range(1, SUBLANES):
        e_ref[r, :, :span, :] = e_ref[0, :, r:r + span, :]

    chunks_per_seq = ts // SUBLANES

    def chunk(c, carry):
        gi = c // chunks_per_seq
        t0 = pl.multiple_of((c % chunks_per_seq) * SUBLANES, SUBLANES)
        acc = jnp.broadcast_to(bdw_ref[...], (SUBLANES, d))
        for k in range(width):
            off = lead + k
            r = off % SUBLANES
            start = pl.multiple_of(t0 + (off - r), SUBLANES)
            acc = acc + e_ref[r, gi, pl.ds(start, SUBLANES), :] * wdw_ref[k:k + 1, :]
        y_ref[gi, pl.ds(t0, SUBLANES), :] = acc
        return carry

    lax.fori_loop(0, g * chunks_per_seq, chunk, 0)

    y = y_ref[...].reshape(g * ts, d)
    mu = jnp.mean(y, axis=-1, keepdims=True)
    var = jnp.mean(jnp.square(y - mu), axis=-1, keepdims=True)
    z = (y - mu) * lax.rsqrt(var + LN_EPS) * lng_ref[...] + lnb_ref[...]
    z = (z * jax.nn.sigmoid(z)).astype(BF16)
    m = _dot(z, w2_ref[...]) + b2_ref[...]
    h = h_ref[...].reshape(g * ts, d)
    out_ref[...] = (h + _rms(m, gpost_ref[...])).reshape(g, ts, d)


def _conv_b(h, u, prev, gains, gi_post, w_dw, b_dw, ln_g, ln_b, w_pw2, b_pw2, lj, group, ts):
    b, s, d = u.shape
    width = w_dw.shape[1]
    from_prev_tile = prev is None
    if from_prev_tile:
        assert ts % CONV_HALO == 0 and group == 1
        per = ts // CONV_HALO
        prev_arg = u
        prev_spec = pl.BlockSpec((group, CONV_HALO, d), lambda i, j: (i, jnp.maximum(j * per - 1, 0), 0))
    else:
        assert ts == s
        prev_arg = prev
        prev_spec = pl.BlockSpec((group, width - 1, d), lambda i, j: (i, 0, 0))
    vec = lambda: pl.BlockSpec((None, 1, d), lambda i, j: (lj, 0, 0))
    return pl.pallas_call(
        functools.partial(_conv_b_kernel, width=width, from_prev_tile=from_prev_tile),
        grid=(b // group, s // ts),
        in_specs=[
            pl.BlockSpec((group, ts, d), lambda i, j: (i, j, 0)),
            pl.BlockSpec((group, ts, d), lambda i, j: (i, j, 0)),
            prev_spec,
            pl.BlockSpec((None, width, d), lambda i, j: (lj, 0, 0)),
            vec(), vec(), vec(),
            pl.BlockSpec((None, d, d), lambda i, j: (lj, 0, 0)),
            vec(),
            _gain_spec(gi_post, d, 2),
        ],
        out_specs=pl.BlockSpec((group, ts, d), lambda i, j: (i, j, 0)),
        out_shape=jax.ShapeDtypeStruct((b, s, d), F32),
        scratch_shapes=[pltpu.VMEM((SUBLANES, group, CONV_HALO + ts, d), F32),
                        pltpu.VMEM((group, ts, d), F32)],
        compiler_params=_cparams("parallel", "arbitrary"),
    )(h, u, prev_arg, w_dw, b_dw, ln_g, ln_b, w_pw2, b_pw2, gains)


def _mla_down(h_ref, g_ref, win_ref, qg_ref, kvg_ref, tab_ref, lat_ref, kpe_ref):
    xn = _rms(h_ref[...], g_ref[...]).astype(BF16)
    down = _dot(xn, win_ref[...])
    c_q = _rms(down[:, :Q_LORA], qg_ref[...])
    c_kv = _rms(down[:, Q_LORA:Q_LORA + KV_LORA], kvg_ref[...])
    base = Q_LORA + KV_LORA
    kpe = (down[:, base:base + LANES] * tab_ref[0] + down[:, base + LANES:base + 2 * LANES] * tab_ref[1])
    lat_ref[...] = c_kv
    kpe_ref[...] = kpe
    return c_q.astype(BF16), c_kv.astype(BF16), kpe


def _mla_pre_prompt_kernel(h_ref, g_ref, win_ref, qg_ref, kvg_ref, tab_ref, wqa_ref, wqb_ref, wk_ref, wv_ref,
                           lat_ref, kpe_ref, q_ref, k_ref, v_ref, *, scale):
    c_q, c_kv, kpe = _mla_down(h_ref, g_ref, win_ref, qg_ref, kvg_ref, tab_ref, lat_ref, kpe_ref)
    qa = _dot(c_q, wqa_ref[...])
    qb = _dot(c_q, wqb_ref[...])
    ka = _dot(c_kv, wk_ref[...])
    v_ref[...] = _dot(c_kv, wv_ref[...]).astype(BF16)
    cos = tab_ref[0] * scale
    sin = tab_ref[1] * scale
    for hh in range(q_ref.shape[0]):
        sl = slice(hh * LANES, (hh + 1) * LANES)
        q_ref[hh] = (qa[:, sl] * cos + qb[:, sl] * sin).astype(BF16)
        k_ref[hh] = (ka[:, sl] + kpe).astype(BF16)


def _mla_pre_sample_kernel(h_ref, g_ref, win_ref, qg_ref, kvg_ref, tab_ref, tabq_ref, wqn_ref, wqa_ref, wqb_ref,
                           wuk_ref, lat_ref, kpe_ref, qlat_ref, qpe_ref, *, scale):
    c_q, _, _ = _mla_down(h_ref, g_ref, win_ref, qg_ref, kvg_ref, tab_ref, lat_ref, kpe_ref)
    qn = _dot(c_q, wqn_ref[...]).astype(BF16)
    qpe = _dot(c_q, wqa_ref[...]) * tabq_ref[0] + _dot(c_q, wqb_ref[...]) * tabq_ref[1]
    qpe_ref[...] = qpe * scale
    nope = wuk_ref.shape[1]
    for hh in range(qlat_ref.shape[0]):
        qlat_ref[hh] = _dot(qn[:, hh * nope:(hh + 1) * nope], wuk_ref[hh]) * scale


def _full(a):
    nd = a.ndim
    return pl.BlockSpec(a.shape, lambda i: (0,) * nd)


def _mla_pre_prompt(h, gains, gi, w, tab, tm):
    t, d = h.shape
    heads = w["wqa"].shape[1] // LANES
    nper = tab.shape[1] // tm
    scale = float((QK_NOPE + QK_ROPE) ** -0.5)
    row = lambda width: pl.BlockSpec((tm, width), lambda i: (i, 0))
    return pl.pallas_call(
        functools.partial(_mla_pre_prompt_kernel, scale=scale),
        grid=(t // tm,),
        in_specs=[
            row(d), _gain_spec(gi, d, 1), _full(w["win"]), _full(w["qg"]), _full(w["kvg"]),
            pl.BlockSpec((2, tm, LANES), lambda i: (0, i % nper, 0)),
            _full(w["wqa"]), _full(w["wqb"]), _full(w["wk"]), _full(w["wv"]),
        ],
        out_specs=[
            row(KV_LORA), row(LANES),
            pl.BlockSpec((heads, tm, LANES), lambda i: (0, i, 0)),
            pl.BlockSpec((heads, tm, LANES), lambda i: (0, i, 0)),
            row(w["wv"].shape[1]),
        ],
        out_shape=[
            jax.ShapeDtypeStruct((t, KV_LORA), F32), jax.ShapeDtypeStruct((t, LANES), F32),
            jax.ShapeDtypeStruct((heads, t, LANES), BF16), jax.ShapeDtypeStruct((heads, t, LANES), BF16),
            jax.ShapeDtypeStruct((t, w["wv"].shape[1]), BF16),
        ],
        compiler_params=_cparams("parallel"),
    )(h, gains, w["win"], w["qg"], w["kvg"], tab, w["wqa"], w["wqb"], w["wk"], w["wv"])


def _mla_pre_sample(h, gains, gi, w, tab, tabq, tm):
    t, d = h.shape
    heads = w["wuk"].shape[0]
    pe_w = w["wqpa"].shape[1]
    scale = float((QK_NOPE + QK_ROPE) ** -0.5)
    row = lambda width: pl.BlockSpec((tm, width), lambda i: (i, 0))
    return pl.pallas_call(
        functools.partial(_mla_pre_sample_kernel, scale=scale),
        grid=(t // tm,),
        in_specs=[
            row(d), _gain_spec(gi, d, 1), _full(w["win"]), _full(w["qg"]), _full(w["kvg"]),
            _full(tab), _full(tabq),
            _full(w["wqn"]), _full(w["wqpa"]), _full(w["wqpb"]), _full(w["wuk"]),
        ],
        out_specs=[
            row(KV_LORA), row(LANES),
            pl.BlockSpec((heads, tm, KV_LORA), lambda i: (0, i, 0)),
            row(pe_w),
        ],
        out_shape=[
            jax.ShapeDtypeStruct((t, KV_LORA), F32), jax.ShapeDtypeStruct((t, LANES), F32),
            jax.ShapeDtypeStruct((heads, t, KV_LORA), F32), jax.ShapeDtypeStruct((t, pe_w), F32),
        ],
        compiler_params=_cparams("parallel"),
    )(h, gains, w["win"], w["qg"], w["kvg"], tab, tabq, w["wqn"], w["wqpa"], w["wqpb"], w["wuk"])


def _flash_kernel(q_ref, k_ref, v_ref, o_ref, m_ref, l_ref, acc_ref, *, tk, vh):
    qi = pl.program_id(2)
    tq = q_ref.shape[1]
    res = []
    for hh in range(q_ref.shape[0]):
        q = q_ref[hh]
        m_ref[...] = jnp.full(m_ref.shape, NEG_INF, F32)
        l_ref[...] = jnp.zeros(l_ref.shape, F32)
        acc_ref[...] = jnp.zeros(acc_ref.shape, F32)

        def step(kj, masked, hh=hh, q=q):
            k0 = pl.multiple_of(kj * tk, tk)
            s = _dot_nt(q, k_ref[hh, pl.ds(k0, tk), :])
            if masked:
                row = lax.broadcasted_iota(jnp.int32, s.shape, 0)
                col = lax.broadcasted_iota(jnp.int32, s.shape, 1)
                s = jnp.where(col <= row, s, NEG_INF)
            m_prev = m_ref[...]
            m_new = jnp.maximum(m_prev, jnp.max(s, axis=-1, keepdims=True))
            alpha = jnp.exp(m_prev - m_new)
            p = jnp.exp(s - m_new)
            l_ref[...] = alpha * l_ref[...] + jnp.sum(p, axis=-1, keepdims=True)
            acc_ref[...] = alpha * acc_ref[...] + _dot(p.astype(BF16), v_ref[pl.ds(k0, tk), :])
            m_ref[...] = m_new

        def body(kj, carry):
            step(kj, False)
            return carry

        lax.fori_loop(0, qi, body, 0)
        step(qi, True)
        res.append(acc_ref[...] / l_ref[...])
    lane = lax.broadcasted_iota(jnp.int32, (tq, LANES), 1)
    o_ref[...] = jnp.where(lane < vh, res[0], res[1]).astype(BF16)


def _flash_prompt(q, k, v, batch, seq, tq):
    heads, t, _ = q.shape
    vh = v.shape[1] // heads
    per = LANES // vh
    assert per == 2
    nq = seq // tq
    return pl.pallas_call(
        functools.partial(_flash_kernel, tk=tq, vh=vh),
        grid=(batch, heads // per, nq),
        in_specs=[
            pl.BlockSpec((per, tq, LANES), lambda b, p, i: (p, b * nq + i, 0)),
            pl.BlockSpec((per, seq, LANES), lambda b, p, i: (p, b, 0)),
            pl.BlockSpec((seq, LANES), lambda b, p, i: (b, p)),
        ],
        out_specs=pl.BlockSpec((tq, LANES), lambda b, p, i: (b * nq + i, p)),
        out_shape=jax.ShapeDtypeStruct((t, heads * vh), BF16),
        scratch_shapes=[pltpu.VMEM((tq, 1), F32), pltpu.VMEM((tq, 1), F32), pltpu.VMEM((tq, LANES), F32)],
        compiler_params=_cparams("parallel", "parallel", "arbitrary"),
    )(q, k, v)


def _decode_kernel(pt_ref, qlat_ref, qpe_ref, cnew_ref, knew_ref, *rest, pages):
    lat_refs = rest[:pages]
    kr_refs = rest[pages:2 * pages]
    o_ref, qpe_s, m_ref, l_ref, acc_ref = rest[2 * pages:]
    j = pl.program_id(1)
    heads, nq, lat_w = qlat_ref.shape
    rows = heads * nq
    rope = knew_ref.shape[-1]

    @pl.when(j == 0)
    def _():
        m_ref[...] = jnp.full(m_ref.shape, NEG_INF, F32)
        l_ref[...] = jnp.zeros(l_ref.shape, F32)
        acc_ref[...] = jnp.zeros(acc_ref.shape, F32)
        for hh in range(heads):
            qpe_s[hh * nq:(hh + 1) * nq, :] = qpe_ref[:, hh * rope:(hh + 1) * rope]

    q = qlat_ref[...].reshape(rows, lat_w).astype(BF16)
    qp = qpe_s[...].astype(BF16)

    def update(s, values):
        m_prev = m_ref[...]
        m_new = jnp.maximum(m_prev, jnp.max(s, axis=-1, keepdims=True))
        alpha = jnp.exp(m_prev - m_new)
        p = jnp.exp(s - m_new)
        l_ref[...] = alpha * l_ref[...] + jnp.sum(p, axis=-1, keepdims=True)
        pv = None
        for sl, lat in values:
            part = _dot(p[:, sl].astype(BF16), lat)
            pv = part if pv is None else pv + part
        acc_ref[...] = alpha * acc_ref[...] + pv
        m_ref[...] = m_new

    lats = [r[...].astype(BF16) for r in lat_refs]
    scores = [_dot_nt(q, lat) + _dot_nt(qp, kr[...].astype(BF16)) for lat, kr in zip(lats, kr_refs)]
    psz = lats[0].shape[0]
    update(jnp.concatenate(scores, axis=-1),
           [(slice(n * psz, (n + 1) * psz), lat) for n, lat in enumerate(lats)])

    @pl.when(j == pl.num_programs(1) - 1)
    def _():
        cnew = jnp.concatenate([cnew_ref[...], jnp.zeros((psz - nq, lat_w), F32)], axis=0).astype(BF16)
        knew = jnp.concatenate([knew_ref[...], jnp.zeros((psz - nq, rope), F32)], axis=0).astype(BF16)
        s = _dot_nt(q, cnew) + _dot_nt(qp, knew)
        qpos = lax.broadcasted_iota(jnp.int32, s.shape, 0) % nq
        kpos = lax.broadcasted_iota(jnp.int32, s.shape, 1)
        update(jnp.where(kpos <= qpos, s, NEG_INF), [(slice(0, psz), cnew)])
        o_ref[...] = acc_ref[...] / l_ref[...]


def _decode_sample(page_table, qlat, qpe, cnew, knew, lat_pool, kr_pool, pages):
    heads, t, lat_w = qlat.shape
    nb, npages = page_table.shape
    nq = t // nb
    psz = lat_pool.shape[1]
    rope = kr_pool.shape[2]
    rows = heads * nq
    pt = page_table.reshape(-1)

    def page_spec(width, n):
        return pl.BlockSpec((None, psz, width), lambda b, j, pt_ref: (pt_ref[b * npages + j * pages + n], 0, 0))

    grid_spec = pltpu.PrefetchScalarGridSpec(
        num_scalar_prefetch=1,
        grid=(nb, npages // pages),
        in_specs=[
            pl.BlockSpec((heads, nq, lat_w), lambda b, j, pt_ref: (0, b, 0)),
            pl.BlockSpec((nq, qpe.shape[1]), lambda b, j, pt_ref: (b, 0)),
            pl.BlockSpec((nq, lat_w), lambda b, j, pt_ref: (b, 0)),
            pl.BlockSpec((nq, rope), lambda b, j, pt_ref: (b, 0)),
        ] + [page_spec(lat_w, n) for n in range(pages)] + [page_spec(rope, n) for n in range(pages)],
        out_specs=pl.BlockSpec((None, rows, lat_w), lambda b, j, pt_ref: (b, 0, 0)),
        scratch_shapes=[pltpu.VMEM((rows, rope), F32), pltpu.VMEM((rows, 1), F32),
                        pltpu.VMEM((rows, 1), F32), pltpu.VMEM((rows, lat_w), F32)],
    )
    return pl.pallas_call(
        functools.partial(_decode_kernel, pages=pages),
        grid_spec=grid_spec,
        out_shape=jax.ShapeDtypeStruct((nb, rows, lat_w), F32),
        compiler_params=_cparams("parallel", "arbitrary"),
    )(pt, qlat, qpe, cnew, knew, *([lat_pool] * pages), *([kr_pool] * pages))


def _mla_post_prompt_kernel(o_ref, h_ref, wo_ref, g_ref, out_ref):
    out_ref[...] = h_ref[...] + _rms(_dot(o_ref[...], wo_ref[...]), g_ref[...])


def _mla_post_prompt(o, h, gains, gi, w_o):
    t, d = h.shape
    tm = _row_tile(t)
    return pl.pallas_call(
        _mla_post_prompt_kernel,
        grid=(t // tm,),
        in_specs=[
            pl.BlockSpec((tm, o.shape[1]), lambda i: (i, 0)),
            pl.BlockSpec((tm, d), lambda i: (i, 0)),
            _full(w_o), _gain_spec(gi, d, 1),
        ],
        out_specs=pl.BlockSpec((tm, d), lambda i: (i, 0)),
        out_shape=jax.ShapeDtypeStruct((t, d), F32),
        compiler_params=_cparams("parallel"),
    )(o, h, w_o, gains)


def _mla_post_sample_kernel(olat_ref, h_ref, wuv_ref, wo_ref, g_ref, out_ref):
    gb, rows, lat_w = olat_ref.shape
    heads = wuv_ref.shape[0]
    nq = rows // heads
    acc = None
    for hh in range(heads):
        x = olat_ref[:, hh * nq:(hh + 1) * nq, :].reshape(gb * nq, lat_w).astype(BF16)
        oh = _dot(x, wuv_ref[hh]).astype(BF16)
        part = _dot(oh, wo_ref[hh])
        acc = part if acc is None else acc + part
    out_ref[...] = h_ref[...] + _rms(acc, g_ref[...])


def _mla_post_sample(olat, h, gains, gi, w_uv, w_o, gb):
    nb, rows, lat_w = olat.shape
    t, d = h.shape
    nq = t // nb
    return pl.pallas_call(
        _mla_post_sample_kernel,
        grid=(nb // gb,),
        in_specs=[
            pl.BlockSpec((gb, rows, lat_w), lambda i: (i, 0, 0)),
            pl.BlockSpec((gb * nq, d), lambda i: (i, 0)),
            _full(w_uv), _full(w_o), _gain_spec(gi, d, 1),
        ],
        out_specs=pl.BlockSpec((gb * nq, d), lambda i: (i, 0)),
        out_shape=jax.ShapeDtypeStruct((t, d), F32),
        compiler_params=_cparams("parallel"),
    )(olat, h, w_uv, w_o, gains)


def _rot_cols(w):
    half = w.shape[-1] // 2
    return jnp.concatenate([-w[..., half:], w[..., :half]], axis=-1)


def _rope_tables(pos, reps):
    inv_freq = ROPE_THETA ** (-jnp.arange(0, QK_ROPE, 2, dtype=F32) / QK_ROPE)
    ang = pos.astype(F32)[:, None] * inv_freq[None, :]
    cos, sin = jnp.cos(ang), jnp.sin(ang)
    p = pos.shape[0]
    pad = LANES - QK_NOPE - QK_ROPE
    cos_g = jnp.concatenate([jnp.ones((p, QK_NOPE), F32), cos, cos, jnp.ones((p, pad), F32)], axis=1)
    sin_g = jnp.concatenate([jnp.zeros((p, QK_NOPE), F32), sin, sin, jnp.zeros((p, pad), F32)], axis=1)
    tab = jnp.stack([cos_g, sin_g])
    cos_q = jnp.tile(jnp.concatenate([cos, cos], axis=1), (1, reps))
    sin_q = jnp.tile(jnp.concatenate([sin, sin], axis=1), (1, reps))
    return tab, jnp.stack([cos_q, sin_q])


def _mla_weights(w_in, q_g, kv_g, w_uq, w_ukv, w_o):
    d = w_in.shape[0]
    heads = w_uq.shape[1]
    pad = LANES - QK_NOPE - QK_ROPE
    wq, wkv, wpe = w_in[:, :Q_LORA], w_in[:, Q_LORA:Q_LORA + KV_LORA], w_in[:, Q_LORA + KV_LORA:]
    zn, zp = jnp.zeros((d, QK_NOPE), F32), jnp.zeros((d, pad), F32)
    win = jnp.concatenate([wq, wkv, zn, wpe, zp, zn, _rot_cols(wpe), zp], axis=1).astype(BF16)
    w_nope, w_pe = w_uq[..., :QK_NOPE], w_uq[..., QK_NOPE:]
    w_pe_rot = _rot_cols(w_pe)
    w_uk, w_uv = w_ukv[..., :QK_NOPE], w_ukv[..., QK_NOPE:]
    vh = w_uv.shape[-1]
    zq_n = jnp.zeros((Q_LORA, heads, QK_NOPE), F32)
    zq_p = jnp.zeros((Q_LORA, heads, pad), F32)
    zk = jnp.zeros((KV_LORA, heads, LANES - QK_NOPE), F32)
    return dict(
        win=win, qg=q_g[None, :], kvg=kv_g[None, :],
        wqa=jnp.concatenate([w_nope, w_pe, zq_p], axis=-1).reshape(Q_LORA, heads * LANES).astype(BF16),
        wqb=jnp.concatenate([zq_n, w_pe_rot, zq_p], axis=-1).reshape(Q_LORA, heads * LANES).astype(BF16),
        wk=jnp.concatenate([w_uk, zk], axis=-1).reshape(KV_LORA, heads * LANES).astype(BF16),
        wv=w_uv.reshape(KV_LORA, heads * vh).astype(BF16),
        wo=w_o.astype(BF16),
        wqn=w_nope.reshape(Q_LORA, heads * QK_NOPE).astype(BF16),
        wqpa=w_pe.reshape(Q_LORA, heads * QK_ROPE).astype(BF16),
        wqpb=w_pe_rot.reshape(Q_LORA, heads * QK_ROPE).astype(BF16),
        wuk=jnp.transpose(w_uk, (1, 2, 0)).astype(BF16),
        wuv=jnp.transpose(w_uv, (1, 0, 2)).astype(BF16),
        wo_h=w_o.reshape(heads, vh, w_o.shape[1]).astype(BF16),
    )


def _trunk(x, is_prompt, W, conv_prevs, mla_pasts, page_table, mem_k, mem_v, mem_rows_per_layer):
    b, s, d = x.shape
    t = b * s
    h = x.reshape(t, d)
    depth = W["ffn_w_out"].shape[0] // 2
    gains = W["gains"]
    heads = W["mla"][0]["wuk"].shape[0]
    tm = _row_tile(t)
    if is_prompt:
        seq_tile = _row_tile(s)
        tab, _ = _rope_tables(jnp.arange(s, dtype=jnp.int32), heads)
        xa_group, xa_ts = 1, seq_tile
    else:
        past_len = page_table.shape[1] * mla_pasts[0][0].shape[1]
        pos = past_len + jnp.arange(s, dtype=jnp.int32)
        tab, tabq = _rope_tables(jnp.tile(pos, tm // s), heads)
        xa_group, xa_ts = min(XATTN_GROUP, b), s
    conv_new, lat_new, rope_new = [], [], []
    for i in range(depth):
        g0 = i * 8
        h = _ffn(h, gains, g0 + 0, g0 + 1, W["ffn_w_in"], W["ffn_w_out"], 2 * i)
        j = i // 2
        if i % 2 == 0:
            u = _conv_a(h, gains, g0 + 2, W["conv_w_pw1"], W["conv_b_pw1"], j)
            u3, h3 = u.reshape(b, s, d), h.reshape(b, s, d)
            cw = W["conv_w_dw"].shape[1]
            if is_prompt:
                hn = _conv_b(h3, u3, None, gains, g0 + 3, W["conv_w_dw"], W["conv_b_dw"], W["conv_ln_g"],
                             W["conv_ln_b"], W["conv_w_pw2"], W["conv_b_pw2"], j, 1, seq_tile)
                conv_new.append(u3[:, s - (cw - 1):])
            else:
                prev = conv_prevs[j]
                hn = _conv_b(h3, u3, prev, gains, g0 + 3, W["conv_w_dw"], W["conv_b_dw"], W["conv_ln_g"],
                             W["conv_ln_b"], W["conv_w_pw2"], W["conv_b_pw2"], j, min(CONV_GROUP, b), s)
                conv_new.append(jnp.concatenate([prev, u3], axis=1)[:, -(cw - 1):])
            h = hn.reshape(t, d)
        else:
            w = W["mla"][j]
            if is_prompt:
                lat, kpe, q, k, v = _mla_pre_prompt(h, gains, g0 + 2, w, tab, seq_tile)
                o = _flash_prompt(q, k, v, b, s, seq_tile)
                h = _mla_post_prompt(o, h, gains, g0 + 3, w["wo"])
            else:
                lat, kpe, qlat, qpe = _mla_pre_sample(h, gains, g0 + 2, w, tab, tabq, tm)
                knew = kpe[:, QK_NOPE:QK_NOPE + QK_ROPE]
                lat_pool, kr_pool = mla_pasts[j]
                olat = _decode_sample(page_table, qlat, qpe, lat, knew, lat_pool, kr_pool,
                                      min(PAGES_PER_STEP, page_table.shape[1]))
                h = _mla_post_sample(olat, h, gains, g0 + 3, w["wuv"], w["wo_h"], min(b, 64))
            lat_new.append(lat.reshape(b, s, KV_LORA))
            rope_new.append(kpe[:, QK_NOPE:QK_NOPE + QK_ROPE].reshape(b, s, QK_ROPE))
        h = _xattn(h.reshape(b, s, d), mem_k, mem_v, i * mem_rows_per_layer, gains, g0 + 4, g0 + 5,
                   W["xa_w_q"], W["xa_w_o"], i, xa_group, xa_ts).reshape(t, d)
        h = _ffn(h, gains, g0 + 6, g0 + 7, W["ffn_w_in"], W["ffn_w_out"], 2 * i + 1)
    return h.reshape(b, s, d), conv_new, lat_new, rope_new


def kernel(x_prompt, x_sample, state_conv_l0, state_conv_l2, cache_mla_latent_l1, cache_mla_krope_l1, cache_mla_latent_l3, cache_mla_krope_l3, cache_mem_k, cache_mem_v, page_table, mem_prompt, norm_gain, ffn_w_in, ffn_w_out, conv_w_pw1, conv_b_pw1, conv_w_dw, conv_b_dw, conv_ln_g, conv_ln_b, conv_w_pw2, conv_b_pw2, mla_w_in, mla_q_norm, mla_kv_norm, mla_w_uq, mla_w_ukv, mla_w_o, xa_mem_norm, xa_w_q, xa_w_kv, xa_w_o):
    depth, _, d = norm_gain.shape
    dff = ffn_w_out.shape[2]
    W = dict(
        gains=norm_gain.reshape(depth * 8, 1, d),
        ffn_w_in=ffn_w_in.reshape(depth * 2, d, 2 * dff).astype(BF16),
        ffn_w_out=ffn_w_out.reshape(depth * 2, dff, d).astype(BF16),
        conv_w_pw1=conv_w_pw1.astype(BF16), conv_b_pw1=conv_b_pw1[:, None, :],
        conv_w_dw=conv_w_dw, conv_b_dw=conv_b_dw[:, None, :],
        conv_ln_g=conv_ln_g[:, None, :], conv_ln_b=conv_ln_b[:, None, :],
        conv_w_pw2=conv_w_pw2.astype(BF16), conv_b_pw2=conv_b_pw2[:, None, :],
        xa_w_q=xa_w_q.astype(BF16), xa_w_o=xa_w_o.astype(BF16),
        mla=[_mla_weights(mla_w_in[j], mla_q_norm[j], mla_kv_norm[j], mla_w_uq[j], mla_w_ukv[j], mla_w_o[j])
             for j in range(mla_w_in.shape[0])],
    )

    bp, n_mem, _ = mem_prompt.shape
    mk, mv = _memkv(mem_prompt.reshape(bp * n_mem, d), xa_mem_norm[:, None, :], xa_w_kv.astype(BF16))
    xw = mk.shape[-1]
    y_prompt, p_conv, p_lat, p_rope = _trunk(
        x_prompt, True, W, None, None, None,
        mk.reshape(depth * bp, n_mem, xw), mv.reshape(depth * bp, n_mem, xw), bp)
    xa_heads = xw // XA_HEAD_DIM
    mem_k_prompt = mk.reshape(depth, bp, n_mem, xa_heads, XA_HEAD_DIM)
    mem_v_prompt = mv.reshape(depth, bp, n_mem, xa_heads, XA_HEAD_DIM)

    bs = x_sample.shape[0]
    y_sample, s_conv, s_lat, s_rope = _trunk(
        x_sample, False, W, [state_conv_l0, state_conv_l2],
        [(cache_mla_latent_l1, cache_mla_krope_l1), (cache_mla_latent_l3, cache_mla_krope_l3)], page_table,
        cache_mem_k.reshape(depth * bs, n_mem, xw), cache_mem_v.reshape(depth * bs, n_mem, xw), bs)

    return (y_prompt, y_sample,
            p_conv[0], p_conv[1], p_lat[0], p_rope[0], p_lat[1], p_rope[1],
            mem_k_prompt, mem_v_prompt,
            s_conv[0], s_conv[1], s_lat[0], s_rope[0], s_lat[1], s_rope[1])
```

```python
import functools

import jax
import jax.numpy as jnp
from jax import lax
from jax.experimental import pallas as pl
from jax.experimental.pallas import tpu as pltpu

F32 = jnp.float32
BF16 = jnp.bfloat16

RMS_EPS = 1e-6
LN_EPS = 1e-5
ROPE_THETA = 10000.0
FFN_RESID = 0.5
NEG_INF = -1e30

Q_LORA = 384
KV_LORA = 256
QK_NOPE = 64
QK_ROPE = 32
XA_HEAD_DIM = 128

LANES = 128
SUBLANES = 8
VMEM_LIMIT_BYTES = 52 * 1024 * 1024

ROW_TILE = 512
CONV_HALO = 32
XATTN_GROUP = 8
CONV_GROUP = 8


def _cparams(*sem):
    return pltpu.CompilerParams(dimension_semantics=sem, vmem_limit_bytes=VMEM_LIMIT_BYTES)


def _row_tile(t):
    return ROW_TILE if t % ROW_TILE == 0 else t


def _rms(x, g):
    return x * lax.rsqrt(jnp.mean(x * x, axis=-1, keepdims=True) + RMS_EPS) * g


def _dot(a, b):
    return jnp.dot(a, b, preferred_element_type=F32)


def _dot_nt(a, b):
    return lax.dot_general(a, b, (((1,), (1,)), ((), ())), preferred_element_type=F32)


def _dot_tn(a, b):
    return lax.dot_general(a, b, (((0,), (0,)), ((), ())), preferred_element_type=F32)


def _gain_spec(idx, d, ngrid):
    if ngrid == 1:
        return pl.BlockSpec((None, 1, d), lambda i: (idx, 0, 0))
    return pl.BlockSpec((None, 1, d), lambda i, j: (idx, 0, 0))


def _ffn_kernel(h_ref, gpre_ref, gpost_ref, wgu_ref, wo_ref, out_ref, xn_ref, acc_ref):
    j = pl.program_id(1)

    @pl.when(j == 0)
    def _():
        xn_ref[...] = _rms(h_ref[...], gpre_ref[...]).astype(BF16)
        acc_ref[...] = jnp.zeros_like(acc_ref)

    tf = wo_ref.shape[0]
    gu = _dot(xn_ref[...], wgu_ref[...])
    gate, up = gu[:, :tf], gu[:, tf:]
    a = (gate * jax.nn.sigmoid(gate) * up).astype(BF16)
    acc_ref[...] += _dot(a, wo_ref[...])

    @pl.when(j == pl.num_programs(1) - 1)
    def _():
        out_ref[...] = h_ref[...] + FFN_RESID * _rms(acc_ref[...], gpost_ref[...])


def _ffn_chunk(f):
    return f // 2 if (f // 2) % LANES == 0 else f


def _ffn_pack_w_in(w_in, f):
    n, d, _ = w_in.shape
    tf = _ffn_chunk(f)
    return w_in.reshape(n, d, 2, f // tf, tf).transpose(0, 1, 3, 2, 4).reshape(n, d, 2 * f)


def _ffn(h, gains, gi_pre, gi_post, w_in, w_out, li):
    t, d = h.shape
    f = w_out.shape[1]
    tm = _row_tile(t)
    tf = _ffn_chunk(f)
    return pl.pallas_call(
        _ffn_kernel,
        grid=(t // tm, f // tf),
        in_specs=[
            pl.BlockSpec((tm, d), lambda i, j: (i, 0)),
            _gain_spec(gi_pre, d, 2),
            _gain_spec(gi_post, d, 2),
            pl.BlockSpec((None, d, 2 * tf), lambda i, j: (li, 0, j)),
            pl.BlockSpec((None, tf, d), lambda i, j: (li, j, 0)),
        ],
        out_specs=pl.BlockSpec((tm, d), lambda i, j: (i, 0)),
        out_shape=jax.ShapeDtypeStruct((t, d), F32),
        scratch_shapes=[pltpu.VMEM((tm, d), BF16), pltpu.VMEM((tm, d), F32)],
        compiler_params=_cparams("parallel", "arbitrary"),
    )(h, gains, gains, w_in, w_out)


def _memkv_kernel(mem_ref, g_ref, w_ref, k_ref, v_ref):
    xn = _rms(mem_ref[...], g_ref[...]).astype(BF16)
    kv = _dot(xn, w_ref[...])
    xw = k_ref.shape[-1]
    k_ref[...] = kv[:, :xw]
    v_ref[...] = kv[:, xw:]


def _memkv(mem, gains, w_kv):
    m, d = mem.shape
    depth, _, xw2 = w_kv.shape
    xw = xw2 // 2
    tm = _row_tile(m)
    out = jax.ShapeDtypeStruct((depth, m, xw), F32)
    return pl.pallas_call(
        _memkv_kernel,
        grid=(depth, m // tm),
        in_specs=[
            pl.BlockSpec((tm, d), lambda l, i: (i, 0)),
            pl.BlockSpec((None, 1, d), lambda l, i: (l, 0, 0)),
            pl.BlockSpec((None, d, xw2), lambda l, i: (l, 0, 0)),
        ],
        out_specs=[pl.BlockSpec((None, tm, xw), lambda l, i: (l, i, 0))] * 2,
        out_shape=[out, out],
        compiler_params=_cparams("parallel", "parallel"),
    )(mem, gains, w_kv)


def _softmax_pv(s, v):
    e = jnp.exp(s - jnp.max(s, axis=-1, keepdims=True))
    l = jnp.sum(e, axis=-1, keepdims=True)
    return jnp.einsum("gqk,gkd->gqd", e.astype(BF16), v, preferred_element_type=F32) / l


def _xattn_kernel(h_ref, k_ref, v_ref, gpre_ref, gpost_ref, wq_ref, wo_ref, out_ref, *, heads):
    g, ts, d = h_ref.shape
    xw = wq_ref.shape[-1]
    hd = xw // heads
    h = h_ref[...].reshape(g * ts, d)
    xn = _rms(h, gpre_ref[...]).astype(BF16)
    q = (_dot(xn, wq_ref[...]) * (hd ** -0.5)).reshape(g, ts, xw)
    k = k_ref[...].astype(BF16)
    v = v_ref[...].astype(BF16)
    if k_ref.shape[-1] == xw:
        q = q.astype(BF16)
        outs = []
        for hh in range(heads):
            sl = slice(hh * hd, (hh + 1) * hd)
            s = jnp.einsum("gqd,gkd->gqk", q[:, :, sl], k[:, :, sl], preferred_element_type=F32)
            outs.append(_softmax_pv(s, v[:, :, sl]))
    else:
        qs = jnp.concatenate([q[:, :, hh * hd:(hh + 1) * hd] for hh in range(heads)], axis=1).astype(BF16)
        s = jnp.einsum("gqd,gkd->gqk", qs, k, preferred_element_type=F32)
        q_head = lax.broadcasted_iota(jnp.int32, s.shape, 1) // ts
        k_head = lax.broadcasted_iota(jnp.int32, s.shape, 2) % heads
        o = _softmax_pv(jnp.where(q_head == k_head, s, NEG_INF), v)
        outs = [o[:, hh * ts:(hh + 1) * ts, :] for hh in range(heads)]
    o = jnp.concatenate(outs, axis=-1).reshape(g * ts, xw).astype(BF16)
    c = _dot(o, wo_ref[...])
    out_ref[...] = (h + _rms(c, gpost_ref[...])).reshape(g, ts, d)


def _xattn(h, k, v, kv_base, gains, gi_pre, gi_post, w_q, w_o, li, group, ts):
    b, s, d = h.shape
    _, m, kw = k.shape
    xw = w_q.shape[-1]
    heads = xw // XA_HEAD_DIM
    kvb = kv_base // group
    return pl.pallas_call(
        functools.partial(_xattn_kernel, heads=heads),
        grid=(b // group, s // ts),
        in_specs=[
            pl.BlockSpec((group, ts, d), lambda i, j: (i, j, 0)),
            pl.BlockSpec((group, m, kw), lambda i, j: (kvb + i, 0, 0)),
            pl.BlockSpec((group, m, kw), lambda i, j: (kvb + i, 0, 0)),
            _gain_spec(gi_pre, d, 2),
            _gain_spec(gi_post, d, 2),
            pl.BlockSpec((None, d, xw), lambda i, j: (li, 0, 0)),
            pl.BlockSpec((None, xw, d), lambda i, j: (li, 0, 0)),
        ],
        out_specs=pl.BlockSpec((group, ts, d), lambda i, j: (i, j, 0)),
        out_shape=jax.ShapeDtypeStruct((b, s, d), F32),
        compiler_params=_cparams("parallel", "parallel"),
    )(h, k, v, gains, gains, w_q, w_o)


def _conv_a_kernel(h_ref, g_ref, w_ref, b_ref, u_ref):
    d = u_ref.shape[-1]
    xn = _rms(h_ref[...], g_ref[...]).astype(BF16)
    y = _dot(xn, w_ref[...]) + b_ref[...]
    u_ref[...] = y[:, :d] * jax.nn.sigmoid(y[:, d:])


def _conv_a(h, gains, gi, w_pw1, b_pw1, lj):
    t, d = h.shape
    tm = _row_tile(t)
    return pl.pallas_call(
        _conv_a_kernel,
        grid=(t // tm,),
        in_specs=[
            pl.BlockSpec((tm, d), lambda i: (i, 0)),
            _gain_spec(gi, d, 1),
            pl.BlockSpec((None, d, 2 * d), lambda i: (lj, 0, 0)),
            pl.BlockSpec((None, 1, 2 * d), lambda i: (lj, 0, 0)),
        ],
        out_specs=pl.BlockSpec((tm, d), lambda i: (i, 0)),
        out_shape=jax.ShapeDtypeStruct((t, d), F32),
        compiler_params=_cparams("parallel"),
    )(h, gains, w_pw1, b_pw1)


def _conv_b_kernel(h_ref, u_ref, prev_ref, wdw_ref, bdw_ref, lng_ref, lnb_ref, w2_ref, b2_ref,
                   gpost_ref, out_ref, e_ref, y_ref, wb_ref, *, width, from_prev_tile):
    g, ts, d = u_ref.shape
    lead = CONV_HALO - (width - 1)
    ext_rows = CONV_HALO + ts

    if from_prev_tile:
        first = pl.program_id(1) == 0

        @pl.when(first)
        def _():
            e_ref[0, :, :CONV_HALO, :] = jnp.zeros((g, CONV_HALO, d), F32)

        @pl.when(jnp.logical_not(first))
        def _():
            e_ref[0, :, :CONV_HALO, :] = prev_ref[...]
    else:
        e_ref[0, :, :lead, :] = jnp.zeros((g, lead, d), F32)
        e_ref[0, :, lead:CONV_HALO, :] = prev_ref[...]
    e_ref[0, :, CONV_HALO:, :] = u_ref[...]

    span = ext_rows - SUBLANES
    for r in range(1, SUBLANES):
        e_ref[r, :, :span, :] = e_ref[0, :, r:r + span, :]

    for k in range(width):
        wb_ref[k] = jnp.broadcast_to(wdw_ref[k:k + 1, :], (SUBLANES, d))

    chunks_per_seq = ts // SUBLANES

    def chunk(c, carry):
        gi = c // chunks_per_seq
        t0 = pl.multiple_of((c % chunks_per_seq) * SUBLANES, SUBLANES)
        acc = jnp.broadcast_to(bdw_ref[...], (SUBLANES, d))
        for k in range(width):
            off = lead + k
            r = off % SUBLANES
            start = pl.multiple_of(t0 + (off - r), SUBLANES)
            acc = acc + e_ref[r, gi, pl.ds(start, SUBLANES), :] * wb_ref[k]
        y_ref[gi, pl.ds(t0, SUBLANES), :] = acc
        return carry

    lax.fori_loop(0, g * chunks_per_seq, chunk, 0)

    y = y_ref[...].reshape(g * ts, d)
    mu = jnp.mean(y, axis=-1, keepdims=True)
    var = jnp.mean(jnp.square(y - mu), axis=-1, keepdims=True)
    z = (y - mu) * lax.rsqrt(var + LN_EPS) * lng_ref[...] + lnb_ref[...]
    z = (z * jax.nn.sigmoid(z)).astype(BF16)
    m = _dot(z, w2_ref[...]) + b2_ref[...]
    h = h_ref[...].reshape(g * ts, d)
    out_ref[...] = (h + _rms(m, gpost_ref[...])).reshape(g, ts, d)


def _conv_b(h, u, prev, gains, gi_post, w_dw, b_dw, ln_g, ln_b, w_pw2, b_pw2, lj, group, ts):
    b, s, d = u.shape
    width = w_dw.shape[1]
    from_prev_tile = prev is None
    if from_prev_tile:
        assert ts % CONV_HALO == 0 and group == 1
        per = ts // CONV_HALO
        prev_arg = u
        prev_spec = pl.BlockSpec((group, CONV_HALO, d), lambda i, j: (i, jnp.maximum(j * per - 1, 0), 0))
    else:
        assert ts == s
        prev_arg = prev
        prev_spec = pl.BlockSpec((group, width - 1, d), lambda i, j: (i, 0, 0))
    vec = lambda: pl.BlockSpec((None, 1, d), lambda i, j: (lj, 0, 0))
    return pl.pallas_call(
        functools.partial(_conv_b_kernel, width=width, from_prev_tile=from_prev_tile),
        grid=(b // group, s // ts),
        in_specs=[
            pl.BlockSpec((group, ts, d), lambda i, j: (i, j, 0)),
            pl.BlockSpec((group, ts, d), lambda i, j: (i, j, 0)),
            prev_spec,
            pl.BlockSpec((None, width, d), lambda i, j: (lj, 0, 0)),
            vec(), vec(), vec(),
            pl.BlockSpec((None, d, d), lambda i, j: (lj, 0, 0)),
            vec(),
            _gain_spec(gi_post, d, 2),
        ],
        out_specs=pl.BlockSpec((group, ts, d), lambda i, j: (i, j, 0)),
        out_shape=jax.ShapeDtypeStruct((b, s, d), F32),
        scratch_shapes=[pltpu.VMEM((SUBLANES, group, CONV_HALO + ts, d), F32),
                        pltpu.VMEM((group, ts, d), F32),
                        pltpu.VMEM((width, SUBLANES, d), F32)],
        compiler_params=_cparams("parallel", "arbitrary"),
    )(h, u, prev_arg, w_dw, b_dw, ln_g, ln_b, w_pw2, b_pw2, gains)


def _mla_down(h_ref, g_ref, win_ref, qg_ref, kvg_ref, tab_ref, lat_ref, kpe_ref):
    xn = _rms(h_ref[...], g_ref[...]).astype(BF16)
    down = _dot(xn, win_ref[...])
    c_q = _rms(down[:, :Q_LORA], qg_ref[...])
    c_kv = _rms(down[:, Q_LORA:Q_LORA + KV_LORA], kvg_ref[...])
    base = Q_LORA + KV_LORA
    kpe = (down[:, base:base + LANES] * tab_ref[0] + down[:, base + LANES:base + 2 * LANES] * tab_ref[1])
    lat_ref[...] = c_kv
    kpe_ref[...] = kpe
    return c_q.astype(BF16), c_kv.astype(BF16), kpe


def _mla_pre_prompt_kernel(h_ref, g_ref, win_ref, qg_ref, kvg_ref, tab_ref, wqa_ref, wqb_ref, wk_ref, wv_ref,
                           lat_ref, kpe_ref, q_ref, k_ref, v_ref, *, scale):
    c_q, c_kv, kpe = _mla_down(h_ref, g_ref, win_ref, qg_ref, kvg_ref, tab_ref, lat_ref, kpe_ref)
    qa = _dot(c_q, wqa_ref[...])
    qb = _dot(c_q, wqb_ref[...])
    ka = _dot(c_kv, wk_ref[...])
    v_ref[...] = _dot(c_kv, wv_ref[...]).astype(BF16)
    cos = tab_ref[0] * scale
    sin = tab_ref[1] * scale
    for hh in range(q_ref.shape[0]):
        sl = slice(hh * LANES, (hh + 1) * LANES)
        q_ref[hh] = (qa[:, sl] * cos + qb[:, sl] * sin).astype(BF16)
        k_ref[hh] = (ka[:, sl] + kpe).astype(BF16)


def _mla_pre_sample_kernel(h_ref, g_ref, win_ref, qg_ref, kvg_ref, tab_ref, tabq_ref, wqn_ref, wqa_ref, wqb_ref,
                           wuk_ref, lat_ref, kpe_ref, qlat_ref, qpe_ref, *, scale):
    c_q, _, _ = _mla_down(h_ref, g_ref, win_ref, qg_ref, kvg_ref, tab_ref, lat_ref, kpe_ref)
    qn = _dot(c_q, wqn_ref[...]).astype(BF16)
    qpe = _dot(c_q, wqa_ref[...]) * tabq_ref[0] + _dot(c_q, wqb_ref[...]) * tabq_ref[1]
    qpe_ref[...] = qpe * scale
    nope = wuk_ref.shape[1]
    for hh in range(qlat_ref.shape[0]):
        qlat_ref[hh] = _dot(qn[:, hh * nope:(hh + 1) * nope], wuk_ref[hh]) * scale


def _full(a):
    nd = a.ndim
    return pl.BlockSpec(a.shape, lambda i: (0,) * nd)


def _mla_pre_prompt(h, gains, gi, w, tab, tm):
    t, d = h.shape
    heads = w["wqa"].shape[1] // LANES
    nper = tab.shape[1] // tm
    scale = float((QK_NOPE + QK_ROPE) ** -0.5)
    row = lambda width: pl.BlockSpec((tm, width), lambda i: (i, 0))
    return pl.pallas_call(
        functools.partial(_mla_pre_prompt_kernel, scale=scale),
        grid=(t // tm,),
        in_specs=[
            row(d), _gain_spec(gi, d, 1), _full(w["win"]), _full(w["qg"]), _full(w["kvg"]),
            pl.BlockSpec((2, tm, LANES), lambda i: (0, i % nper, 0)),
            _full(w["wqa"]), _full(w["wqb"]), _full(w["wk"]), _full(w["wv"]),
        ],
        out_specs=[
            row(KV_LORA), row(LANES),
            pl.BlockSpec((heads, tm, LANES), lambda i: (0, i, 0)),
            pl.BlockSpec((heads, tm, LANES), lambda i: (0, i, 0)),
            row(w["wv"].shape[1]),
        ],
        out_shape=[
            jax.ShapeDtypeStruct((t, KV_LORA), F32), jax.ShapeDtypeStruct((t, LANES), F32),
            jax.ShapeDtypeStruct((heads, t, LANES), BF16), jax.ShapeDtypeStruct((heads, t, LANES), BF16),
            jax.ShapeDtypeStruct((t, w["wv"].shape[1]), BF16),
        ],
        compiler_params=_cparams("parallel"),
    )(h, gains, w["win"], w["qg"], w["kvg"], tab, w["wqa"], w["wqb"], w["wk"], w["wv"])


def _mla_pre_sample(h, gains, gi, w, tab, tabq, tm):
    t, d = h.shape
    heads = w["wuk"].shape[0]
    pe_w = w["wqpa"].shape[1]
    scale = float((QK_NOPE + QK_ROPE) ** -0.5)
    row = lambda width: pl.BlockSpec((tm, width), lambda i: (i, 0))
    return pl.pallas_call(
        functools.partial(_mla_pre_sample_kernel, scale=scale),
        grid=(t // tm,),
        in_specs=[
            row(d), _gain_spec(gi, d, 1), _full(w["win"]), _full(w["qg"]), _full(w["kvg"]),
            _full(tab), _full(tabq),
            _full(w["wqn"]), _full(w["wqpa"]), _full(w["wqpb"]), _full(w["wuk"]),
        ],
        out_specs=[
            row(KV_LORA), row(LANES),
            pl.BlockSpec((heads, tm, KV_LORA), lambda i: (0, i, 0)),
            row(pe_w),
        ],
        out_shape=[
            jax.ShapeDtypeStruct((t, KV_LORA), F32), jax.ShapeDtypeStruct((t, LANES), F32),
            jax.ShapeDtypeStruct((heads, t, KV_LORA), F32), jax.ShapeDtypeStruct((t, pe_w), F32),
        ],
        compiler_params=_cparams("parallel"),
    )(h, gains, w["win"], w["qg"], w["kvg"], tab, tabq, w["wqn"], w["wqpa"], w["wqpb"], w["wuk"])


def _flash_kernel(q_ref, k_ref, v_ref, o_ref, *, vh):
    qi = pl.program_id(2)
    tq = q_ref.shape[1]
    nq = k_ref.shape[1] // tq

    def attend(n):
        past = n * tq
        row = lax.broadcasted_iota(jnp.int32, (tq, tq), 0)
        col = lax.broadcasted_iota(jnp.int32, (tq, tq), 1)
        res = []
        for hh in range(q_ref.shape[0]):
            q = q_ref[hh]
            s_diag = jnp.where(col <= row, _dot_nt(q, k_ref[hh, past:past + tq, :]), NEG_INF)
            m = jnp.max(s_diag, axis=-1, keepdims=True)
            if n:
                s_past = _dot_nt(q, k_ref[hh, :past, :])
                m = jnp.maximum(m, jnp.max(s_past, axis=-1, keepdims=True))
            p_diag = jnp.exp(s_diag - m)
            l = jnp.sum(p_diag, axis=-1, keepdims=True)
            o = _dot(p_diag.astype(BF16), v_ref[past:past + tq, :])
            if n:
                p_past = jnp.exp(s_past - m)
                l = l + jnp.sum(p_past, axis=-1, keepdims=True)
                o = o + _dot(p_past.astype(BF16), v_ref[:past, :])
            res.append(o / l)
        lane = lax.broadcasted_iota(jnp.int32, (tq, LANES), 1)
        o_ref[...] = jnp.where(lane < vh, res[0], res[1]).astype(BF16)

    for n in range(nq):
        pl.when(qi == n)(functools.partial(attend, n))


def _flash_prompt(q, k, v, batch, seq, tq):
    heads, t, _ = q.shape
    vh = v.shape[1] // heads
    per = LANES // vh
    assert per == 2
    nq = seq // tq
    return pl.pallas_call(
        functools.partial(_flash_kernel, vh=vh),
        grid=(batch, heads // per, nq),
        in_specs=[
            pl.BlockSpec((per, tq, LANES), lambda b, p, i: (p, b * nq + i, 0)),
            pl.BlockSpec((per, seq, LANES), lambda b, p, i: (p, b, 0)),
            pl.BlockSpec((seq, LANES), lambda b, p, i: (b, p)),
        ],
        out_specs=pl.BlockSpec((tq, LANES), lambda b, p, i: (b * nq + i, p)),
        out_shape=jax.ShapeDtypeStruct((t, heads * vh), BF16),
        compiler_params=_cparams("parallel", "parallel", "arbitrary"),
    )(q, k, v)


def _decode_kernel(pt_ref, qlat_ref, qpe_ref, cnew_ref, knew_ref, lat_hbm, kr_hbm, o_ref,
                   qpe_s, lat_buf, kr_buf, sem, *, pages):
    b = pl.program_id(0)
    slot = b % 2
    heads, nq, lat_w = qlat_ref.shape
    rows = heads * nq
    rope = knew_ref.shape[-1]
    psz = lat_hbm.shape[1]
    past = pages * psz

    def page_copies(seq, to_slot, n):
        page = pt_ref[seq * pages + n]
        dst = pl.ds(pl.multiple_of(n * psz, psz), psz)
        return (pltpu.make_async_copy(lat_hbm.at[page], lat_buf.at[to_slot, dst, :], sem.at[0, to_slot]),
                pltpu.make_async_copy(kr_hbm.at[page], kr_buf.at[to_slot, dst, :], sem.at[1, to_slot]))

    def for_each_page(seq, to_slot, act):
        def body(n, carry):
            for cp in page_copies(seq, to_slot, n):
                act(cp)
            return carry
        lax.fori_loop(0, pages, body, 0)

    @pl.when(b == 0)
    def _():
        for_each_page(0, 0, lambda cp: cp.start())

    @pl.when(b + 1 < pl.num_programs(0))
    def _():
        for_each_page(b + 1, 1 - slot, lambda cp: cp.start())

    for hh in range(heads):
        qpe_s[hh * nq:(hh + 1) * nq, :] = qpe_ref[:, hh * rope:(hh + 1) * rope]
    q = qlat_ref[...].reshape(rows, lat_w).astype(BF16)
    qp = qpe_s[...].astype(BF16)

    lat_buf[slot, past:, :] = jnp.concatenate([cnew_ref[...], jnp.zeros((psz - nq, lat_w), F32)], axis=0)
    kr_buf[slot, past:, :] = jnp.concatenate([knew_ref[...], jnp.zeros((psz - nq, rope), F32)], axis=0)

    for_each_page(b, slot, lambda cp: cp.wait())

    lat_all = lat_buf[slot].astype(BF16)
    kr_all = kr_buf[slot].astype(BF16)
    s = _dot_nt(lat_all, q) + _dot_nt(kr_all, qp)
    kpos = lax.broadcasted_iota(jnp.int32, (psz, rows), 0)
    qpos = lax.broadcasted_iota(jnp.int32, (psz, rows), 1) % nq
    s = jnp.concatenate([s[:past], jnp.where(kpos <= qpos, s[past:], NEG_INF)], axis=0)
    p = jnp.exp(s - jnp.max(s, axis=0, keepdims=True))
    p = (p * (1.0 / jnp.sum(p, axis=0, keepdims=True))).astype(BF16)
    o_ref[...] = _dot_tn(p, lat_all)


def _decode_sample(page_table, qlat, qpe, cnew, knew, lat_pool, kr_pool):
    heads, t, lat_w = qlat.shape
    nb, pages = page_table.shape
    nq = t // nb
    psz = lat_pool.shape[1]
    rope = kr_pool.shape[2]
    rows = heads * nq
    keys = (pages + 1) * psz
    grid_spec = pltpu.PrefetchScalarGridSpec(
        num_scalar_prefetch=1,
        grid=(nb,),
        in_specs=[
            pl.BlockSpec((heads, nq, lat_w), lambda b, pt_ref: (0, b, 0)),
            pl.BlockSpec((nq, qpe.shape[1]), lambda b, pt_ref: (b, 0)),
            pl.BlockSpec((nq, lat_w), lambda b, pt_ref: (b, 0)),
            pl.BlockSpec((nq, rope), lambda b, pt_ref: (b, 0)),
            pl.BlockSpec(memory_space=pl.ANY),
            pl.BlockSpec(memory_space=pl.ANY),
        ],
        out_specs=pl.BlockSpec((None, rows, lat_w), lambda b, pt_ref: (b, 0, 0)),
        scratch_shapes=[pltpu.VMEM((rows, rope), F32),
                        pltpu.VMEM((2, keys, lat_w), F32), pltpu.VMEM((2, keys, rope), F32),
                        pltpu.SemaphoreType.DMA((2, 2))],
    )
    return pl.pallas_call(
        functools.partial(_decode_kernel, pages=pages),
        grid_spec=grid_spec,
        out_shape=jax.ShapeDtypeStruct((nb, rows, lat_w), F32),
        compiler_params=_cparams("arbitrary"),
    )(page_table.reshape(-1), qlat, qpe, cnew, knew, lat_pool, kr_pool)


def _mla_post_prompt_kernel(o_ref, h_ref, wo_ref, g_ref, out_ref):
    out_ref[...] = h_ref[...] + _rms(_dot(o_ref[...], wo_ref[...]), g_ref[...])


def _mla_post_prompt(o, h, gains, gi, w_o):
    t, d = h.shape
    tm = _row_tile(t)
    return pl.pallas_call(
        _mla_post_prompt_kernel,
        grid=(t // tm,),
        in_specs=[
            pl.BlockSpec((tm, o.shape[1]), lambda i: (i, 0)),
            pl.BlockSpec((tm, d), lambda i: (i, 0)),
            _full(w_o), _gain_spec(gi, d, 1),
        ],
        out_specs=pl.BlockSpec((tm, d), lambda i: (i, 0)),
        out_shape=jax.ShapeDtypeStruct((t, d), F32),
        compiler_params=_cparams("parallel"),
    )(o, h, w_o, gains)


def _mla_post_sample_kernel(olat_ref, h_ref, wuv_ref, wo_ref, g_ref, out_ref):
    gb, rows, lat_w = olat_ref.shape
    heads = wuv_ref.shape[0]
    nq = rows // heads
    acc = None
    for hh in range(heads):
        x = olat_ref[:, hh * nq:(hh + 1) * nq, :].reshape(gb * nq, lat_w).astype(BF16)
        oh = _dot(x, wuv_ref[hh]).astype(BF16)
        part = _dot(oh, wo_ref[hh])
        acc = part if acc is None else acc + part
    out_ref[...] = h_ref[...] + _rms(acc, g_ref[...])


def _mla_post_sample(olat, h, gains, gi, w_uv, w_o, gb):
    nb, rows, lat_w = olat.shape
    t, d = h.shape
    nq = t // nb
    return pl.pallas_call(
        _mla_post_sample_kernel,
        grid=(nb // gb,),
        in_specs=[
            pl.BlockSpec((gb, rows, lat_w), lambda i: (i, 0, 0)),
            pl.BlockSpec((gb * nq, d), lambda i: (i, 0)),
            _full(w_uv), _full(w_o), _gain_spec(gi, d, 1),
        ],
        out_specs=pl.BlockSpec((gb * nq, d), lambda i: (i, 0)),
        out_shape=jax.ShapeDtypeStruct((t, d), F32),
        compiler_params=_cparams("parallel"),
    )(olat, h, w_uv, w_o, gains)


def _rot_cols(w):
    half = w.shape[-1] // 2
    return jnp.concatenate([-w[..., half:], w[..., :half]], axis=-1)


def _rope_tables(pos, reps):
    inv_freq = ROPE_THETA ** (-jnp.arange(0, QK_ROPE, 2, dtype=F32) / QK_ROPE)
    ang = pos.astype(F32)[:, None] * inv_freq[None, :]
    cos, sin = jnp.cos(ang), jnp.sin(ang)
    p = pos.shape[0]
    pad = LANES - QK_NOPE - QK_ROPE
    cos_g = jnp.concatenate([jnp.ones((p, QK_NOPE), F32), cos, cos, jnp.ones((p, pad), F32)], axis=1)
    sin_g = jnp.concatenate([jnp.zeros((p, QK_NOPE), F32), sin, sin, jnp.zeros((p, pad), F32)], axis=1)
    tab = jnp.stack([cos_g, sin_g])
    cos_q = jnp.tile(jnp.concatenate([cos, cos], axis=1), (1, reps))
    sin_q = jnp.tile(jnp.concatenate([sin, sin], axis=1), (1, reps))
    return tab, jnp.stack([cos_q, sin_q])


def _mla_weights(w_in, q_g, kv_g, w_uq, w_ukv, w_o):
    d = w_in.shape[0]
    heads = w_uq.shape[1]
    pad = LANES - QK_NOPE - QK_ROPE
    wq, wkv, wpe = w_in[:, :Q_LORA], w_in[:, Q_LORA:Q_LORA + KV_LORA], w_in[:, Q_LORA + KV_LORA:]
    zn, zp = jnp.zeros((d, QK_NOPE), F32), jnp.zeros((d, pad), F32)
    win = jnp.concatenate([wq, wkv, zn, wpe, zp, zn, _rot_cols(wpe), zp], axis=1).astype(BF16)
    w_nope, w_pe = w_uq[..., :QK_NOPE], w_uq[..., QK_NOPE:]
    w_pe_rot = _rot_cols(w_pe)
    w_uk, w_uv = w_ukv[..., :QK_NOPE], w_ukv[..., QK_NOPE:]
    vh = w_uv.shape[-1]
    zq_n = jnp.zeros((Q_LORA, heads, QK_NOPE), F32)
    zq_p = jnp.zeros((Q_LORA, heads, pad), F32)
    zk = jnp.zeros((KV_LORA, heads, LANES - QK_NOPE), F32)
    return dict(
        win=win, qg=q_g[None, :], kvg=kv_g[None, :],
        wqa=jnp.concatenate([w_nope, w_pe, zq_p], axis=-1).reshape(Q_LORA, heads * LANES).astype(BF16),
        wqb=jnp.concatenate([zq_n, w_pe_rot, zq_p], axis=-1).reshape(Q_LORA, heads * LANES).astype(BF16),
        wk=jnp.concatenate([w_uk, zk], axis=-1).reshape(KV_LORA, heads * LANES).astype(BF16),
        wv=w_uv.reshape(KV_LORA, heads * vh).astype(BF16),
        wo=w_o.astype(BF16),
        wqn=w_nope.reshape(Q_LORA, heads * QK_NOPE).astype(BF16),
        wqpa=w_pe.reshape(Q_LORA, heads * QK_ROPE).astype(BF16),
        wqpb=w_pe_rot.reshape(Q_LORA, heads * QK_ROPE).astype(BF16),
        wuk=jnp.transpose(w_uk, (1, 2, 0)).astype(BF16),
        wuv=jnp.transpose(w_uv, (1, 0, 2)).astype(BF16),
        wo_h=w_o.reshape(heads, vh, w_o.shape[1]).astype(BF16),
    )


def _trunk(x, is_prompt, W, conv_prevs, mla_pasts, page_table, mem_k, mem_v, mem_rows_per_layer):
    b, s, d = x.shape
    t = b * s
    h = x.reshape(t, d)
    depth = W["ffn_w_out"].shape[0] // 2
    gains = W["gains"]
    heads = W["mla"][0]["wuk"].shape[0]
    tm = _row_tile(t)
    if is_prompt:
        seq_tile = _row_tile(s)
        tab, _ = _rope_tables(jnp.arange(s, dtype=jnp.int32), heads)
        xa_group, xa_ts = 1, seq_tile
    else:
        past_len = page_table.shape[1] * mla_pasts[0][0].shape[1]
        pos = past_len + jnp.arange(s, dtype=jnp.int32)
        tab, tabq = _rope_tables(jnp.tile(pos, tm // s), heads)
        xa_group, xa_ts = min(XATTN_GROUP, b), s
    conv_new, lat_new, rope_new = [], [], []
    for i in range(depth):
        g0 = i * 8
        h = _ffn(h, gains, g0 + 0, g0 + 1, W["ffn_w_in"], W["ffn_w_out"], 2 * i)
        j = i // 2
        if i % 2 == 0:
            u = _conv_a(h, gains, g0 + 2, W["conv_w_pw1"], W["conv_b_pw1"], j)
            u3, h3 = u.reshape(b, s, d), h.reshape(b, s, d)
            cw = W["conv_w_dw"].shape[1]
            if is_prompt:
                hn = _conv_b(h3, u3, None, gains, g0 + 3, W["conv_w_dw"], W["conv_b_dw"], W["conv_ln_g"],
                             W["conv_ln_b"], W["conv_w_pw2"], W["conv_b_pw2"], j, 1, seq_tile)
                conv_new.append(u3[:, s - (cw - 1):])
            else:
                prev = conv_prevs[j]
                hn = _conv_b(h3, u3, prev, gains, g0 + 3, W["conv_w_dw"], W["conv_b_dw"], W["conv_ln_g"],
                             W["conv_ln_b"], W["conv_w_pw2"], W["conv_b_pw2"], j, min(CONV_GROUP, b), s)
                conv_new.append(jnp.concatenate([prev, u3], axis=1)[:, -(cw - 1):])
            h = hn.reshape(t, d)
        else:
            w = W["mla"][j]
            if is_prompt:
                lat, kpe, q, k, v = _mla_pre_prompt(h, gains, g0 + 2, w, tab, seq_tile)
                o = _flash_prompt(q, k, v, b, s, seq_tile)
                h = _mla_post_prompt(o, h, gains, g0 + 3, w["wo"])
            else:
                lat, kpe, qlat, qpe = _mla_pre_sample(h, gains, g0 + 2, w, tab, tabq, tm)
                knew = kpe[:, QK_NOPE:QK_NOPE + QK_ROPE]
                lat_pool, kr_pool = mla_pasts[j]
                olat = _decode_sample(page_table, qlat, qpe, lat, knew, lat_pool, kr_pool)
                h = _mla_post_sample(olat, h, gains, g0 + 3, w["wuv"], w["wo_h"], min(b, 64))
            lat_new.append(lat.reshape(b, s, KV_LORA))
            rope_new.append(kpe[:, QK_NOPE:QK_NOPE + QK_ROPE].reshape(b, s, QK_ROPE))
        h = _xattn(h.reshape(b, s, d), mem_k, mem_v, i * mem_rows_per_layer, gains, g0 + 4, g0 + 5,
                   W["xa_w_q"], W["xa_w_o"], i, xa_group, xa_ts).reshape(t, d)
        h = _ffn(h, gains, g0 + 6, g0 + 7, W["ffn_w_in"], W["ffn_w_out"], 2 * i + 1)
    return h.reshape(b, s, d), conv_new, lat_new, rope_new


def kernel(x_prompt, x_sample, state_conv_l0, state_conv_l2, cache_mla_latent_l1, cache_mla_krope_l1, cache_mla_latent_l3, cache_mla_krope_l3, cache_mem_k, cache_mem_v, page_table, mem_prompt, norm_gain, ffn_w_in, ffn_w_out, conv_w_pw1, conv_b_pw1, conv_w_dw, conv_b_dw, conv_ln_g, conv_ln_b, conv_w_pw2, conv_b_pw2, mla_w_in, mla_q_norm, mla_kv_norm, mla_w_uq, mla_w_ukv, mla_w_o, xa_mem_norm, xa_w_q, xa_w_kv, xa_w_o):
    depth, _, d = norm_gain.shape
    dff = ffn_w_out.shape[2]
    W = dict(
        gains=norm_gain.reshape(depth * 8, 1, d),
        ffn_w_in=_ffn_pack_w_in(ffn_w_in.reshape(depth * 2, d, 2 * dff), dff).astype(BF16),
        ffn_w_out=ffn_w_out.reshape(depth * 2, dff, d).astype(BF16),
        conv_w_pw1=conv_w_pw1.astype(BF16), conv_b_pw1=conv_b_pw1[:, None, :],
        conv_w_dw=conv_w_dw, conv_b_dw=conv_b_dw[:, None, :],
        conv_ln_g=conv_ln_g[:, None, :], conv_ln_b=conv_ln_b[:, None, :],
        conv_w_pw2=conv_w_pw2.astype(BF16), conv_b_pw2=conv_b_pw2[:, None, :],
        xa_w_q=xa_w_q.astype(BF16), xa_w_o=xa_w_o.astype(BF16),
        mla=[_mla_weights(mla_w_in[j], mla_q_norm[j], mla_kv_norm[j], mla_w_uq[j], mla_w_ukv[j], mla_w_o[j])
             for j in range(mla_w_in.shape[0])],
    )

    bp, n_mem, _ = mem_prompt.shape
    mk, mv = _memkv(mem_prompt.reshape(bp * n_mem, d), xa_mem_norm[:, None, :], xa_w_kv.astype(BF16))
    xw = mk.shape[-1]
    y_prompt, p_conv, p_lat, p_rope = _trunk(
        x_prompt, True, W, None, None, None,
        mk.reshape(depth * bp, n_mem, xw), mv.reshape(depth * bp, n_mem, xw), bp)
    xa_heads = xw // XA_HEAD_DIM
    mem_k_prompt = mk.reshape(depth, bp, n_mem, xa_heads, XA_HEAD_DIM)
    mem_v_prompt = mv.reshape(depth, bp, n_mem, xa_heads, XA_HEAD_DIM)

    bs = x_sample.shape[0]
    y_sample, s_conv, s_lat, s_rope = _trunk(
        x_sample, False, W, [state_conv_l0, state_conv_l2],
        [(cache_mla_latent_l1, cache_mla_krope_l1), (cache_mla_latent_l3, cache_mla_krope_l3)], page_table,
        cache_mem_k.reshape(depth * bs, n_mem * xa_heads, XA_HEAD_DIM),
        cache_mem_v.reshape(depth * bs, n_mem * xa_heads, XA_HEAD_DIM), bs)

    return (y_prompt, y_sample,
            p_conv[0], p_conv[1], p_lat[0], p_rope[0], p_lat[1], p_rope[1],
            mem_k_prompt, mem_v_prompt,
            s_conv[0], s_conv[1], s_lat[0], s_rope[0], s_lat[1], s_rope[1])
```

```python
import functools

import jax
import jax.numpy as jnp
from jax import lax
from jax.experimental import pallas as pl
from jax.experimental.pallas import tpu as pltpu

F32 = jnp.float32
BF16 = jnp.bfloat16

RMS_EPS = 1e-6
LN_EPS = 1e-5
ROPE_THETA = 10000.0
FFN_RESID = 0.5
NEG_INF = -1e30

Q_LORA = 384
KV_LORA = 256
QK_NOPE = 64
QK_ROPE = 32
XA_HEAD_DIM = 128

LANES = 128
SUBLANES = 8
VMEM_LIMIT_BYTES = 52 * 1024 * 1024

ROW_TILE = 512
CONV_HALO = 32
DECODE_KEY_CHUNK = 1024
XATTN_GROUP = 8
CONV_GROUP = 8


def _cparams(*sem):
    return pltpu.CompilerParams(dimension_semantics=sem, vmem_limit_bytes=VMEM_LIMIT_BYTES)


def _row_tile(t):
    return ROW_TILE if t % ROW_TILE == 0 else t


def _rms(x, g):
    return x * lax.rsqrt(jnp.mean(x * x, axis=-1, keepdims=True) + RMS_EPS) * g


def _dot(a, b):
    return jnp.dot(a, b, preferred_element_type=F32)


def _dot_nt(a, b):
    return lax.dot_general(a, b, (((1,), (1,)), ((), ())), preferred_element_type=F32)


def _dot_tn(a, b):
    return lax.dot_general(a, b, (((0,), (0,)), ((), ())), preferred_element_type=F32)


def _gain_spec(idx, d, ngrid):
    if ngrid == 1:
        return pl.BlockSpec((None, 1, d), lambda i: (idx, 0, 0))
    return pl.BlockSpec((None, 1, d), lambda i, j: (idx, 0, 0))


def _ffn_kernel(h_ref, gpre_ref, gpost_ref, wgu_ref, wo_ref, out_ref, xn_ref, acc_ref):
    j = pl.program_id(1)

    @pl.when(j == 0)
    def _():
        xn_ref[...] = _rms(h_ref[...], gpre_ref[...]).astype(BF16)
        acc_ref[...] = jnp.zeros_like(acc_ref)

    tf = wo_ref.shape[0]
    gu = _dot(xn_ref[...], wgu_ref[...])
    gate, up = gu[:, :tf], gu[:, tf:]
    a = (gate * jax.nn.sigmoid(gate) * up).astype(BF16)
    acc_ref[...] += _dot(a, wo_ref[...])

    @pl.when(j == pl.num_programs(1) - 1)
    def _():
        out_ref[...] = h_ref[...] + FFN_RESID * _rms(acc_ref[...], gpost_ref[...])


def _ffn_chunk(f):
    return f // 2 if (f // 2) % LANES == 0 else f


def _ffn_pack_w_in(w_in, f):
    tf = _ffn_chunk(f)
    parts = []
    for j in range(f // tf):
        parts += [w_in[..., j * tf:(j + 1) * tf], w_in[..., f + j * tf:f + (j + 1) * tf]]
    return jnp.concatenate(parts, axis=-1)


def _ffn(h, gains, gi_pre, gi_post, w_in, w_out, li):
    t, d = h.shape
    f = w_out.shape[1]
    tm = _row_tile(t)
    tf = _ffn_chunk(f)
    return pl.pallas_call(
        _ffn_kernel,
        grid=(t // tm, f // tf),
        in_specs=[
            pl.BlockSpec((tm, d), lambda i, j: (i, 0)),
            _gain_spec(gi_pre, d, 2),
            _gain_spec(gi_post, d, 2),
            pl.BlockSpec((None, d, 2 * tf), lambda i, j: (li, 0, j)),
            pl.BlockSpec((None, tf, d), lambda i, j: (li, j, 0)),
        ],
        out_specs=pl.BlockSpec((tm, d), lambda i, j: (i, 0)),
        out_shape=jax.ShapeDtypeStruct((t, d), F32),
        scratch_shapes=[pltpu.VMEM((tm, d), BF16), pltpu.VMEM((tm, d), F32)],
        compiler_params=_cparams("parallel", "arbitrary"),
    )(h, gains, gains, w_in, w_out)


def _memkv_kernel(mem_ref, g_ref, w_ref, k_ref, v_ref):
    xn = _rms(mem_ref[...], g_ref[...]).astype(BF16)
    kv = _dot(xn, w_ref[...])
    xw = k_ref.shape[-1]
    k_ref[...] = kv[:, :xw]
    v_ref[...] = kv[:, xw:]


def _memkv(mem, gains, w_kv):
    m, d = mem.shape
    depth, _, xw2 = w_kv.shape
    xw = xw2 // 2
    tm = _row_tile(m)
    out = jax.ShapeDtypeStruct((depth, m, xw), F32)
    return pl.pallas_call(
        _memkv_kernel,
        grid=(depth, m // tm),
        in_specs=[
            pl.BlockSpec((tm, d), lambda l, i: (i, 0)),
            pl.BlockSpec((None, 1, d), lambda l, i: (l, 0, 0)),
            pl.BlockSpec((None, d, xw2), lambda l, i: (l, 0, 0)),
        ],
        out_specs=[pl.BlockSpec((None, tm, xw), lambda l, i: (l, i, 0))] * 2,
        out_shape=[out, out],
        compiler_params=_cparams("parallel", "parallel"),
    )(mem, gains, w_kv)


def _softmax_pv(s, v):
    e = jnp.exp(s - jnp.max(s, axis=-1, keepdims=True))
    l = jnp.sum(e, axis=-1, keepdims=True)
    return jnp.einsum("gqk,gkd->gqd", e.astype(BF16), v, preferred_element_type=F32) / l


def _xattn_kernel(h_ref, k_ref, v_ref, gpre_ref, gpost_ref, wq_ref, wo_ref, out_ref, *, heads):
    g, ts, d = h_ref.shape
    xw = wq_ref.shape[-1]
    hd = xw // heads
    h = h_ref[...].reshape(g * ts, d)
    xn = _rms(h, gpre_ref[...]).astype(BF16)
    q = (_dot(xn, wq_ref[...]) * (hd ** -0.5)).reshape(g, ts, xw)
    k = k_ref[...].astype(BF16)
    v = v_ref[...].astype(BF16)
    if k_ref.shape[-1] == xw:
        q = q.astype(BF16)
        outs = []
        for hh in range(heads):
            sl = slice(hh * hd, (hh + 1) * hd)
            s = jnp.einsum("gqd,gkd->gqk", q[:, :, sl], k[:, :, sl], preferred_element_type=F32)
            outs.append(_softmax_pv(s, v[:, :, sl]))
    else:
        qs = jnp.concatenate([q[:, :, hh * hd:(hh + 1) * hd] for hh in range(heads)], axis=1).astype(BF16)
        s = jnp.einsum("gqd,gkd->gqk", qs, k, preferred_element_type=F32)
        q_head = lax.broadcasted_iota(jnp.int32, s.shape, 1) // ts
        k_head = lax.broadcasted_iota(jnp.int32, s.shape, 2) % heads
        o = _softmax_pv(jnp.where(q_head == k_head, s, NEG_INF), v)
        outs = [o[:, hh * ts:(hh + 1) * ts, :] for hh in range(heads)]
    o = jnp.concatenate(outs, axis=-1).reshape(g * ts, xw).astype(BF16)
    c = _dot(o, wo_ref[...])
    out_ref[...] = (h + _rms(c, gpost_ref[...])).reshape(g, ts, d)


def _xattn(h, k, v, kv_base, gains, gi_pre, gi_post, w_q, w_o, li, group, ts):
    b, s, d = h.shape
    _, m, kw = k.shape
    xw = w_q.shape[-1]
    heads = xw // XA_HEAD_DIM
    kvb = kv_base // group
    return pl.pallas_call(
        functools.partial(_xattn_kernel, heads=heads),
        grid=(b // group, s // ts),
        in_specs=[
            pl.BlockSpec((group, ts, d), lambda i, j: (i, j, 0)),
            pl.BlockSpec((group, m, kw), lambda i, j: (kvb + i, 0, 0)),
            pl.BlockSpec((group, m, kw), lambda i, j: (kvb + i, 0, 0)),
            _gain_spec(gi_pre, d, 2),
            _gain_spec(gi_post, d, 2),
            pl.BlockSpec((None, d, xw), lambda i, j: (li, 0, 0)),
            pl.BlockSpec((None, xw, d), lambda i, j: (li, 0, 0)),
        ],
        out_specs=pl.BlockSpec((group, ts, d), lambda i, j: (i, j, 0)),
        out_shape=jax.ShapeDtypeStruct((b, s, d), F32),
        compiler_params=_cparams("parallel", "parallel"),
    )(h, k, v, gains, gains, w_q, w_o)


def _conv_a_kernel(h_ref, g_ref, w_ref, b_ref, u_ref):
    d = u_ref.shape[-1]
    xn = _rms(h_ref[...], g_ref[...]).astype(BF16)
    y = _dot(xn, w_ref[...]) + b_ref[...]
    u_ref[...] = y[:, :d] * jax.nn.sigmoid(y[:, d:])


def _conv_a(h, gains, gi, w_pw1, b_pw1, lj):
    t, d = h.shape
    tm = _row_tile(t)
    return pl.pallas_call(
        _conv_a_kernel,
        grid=(t // tm,),
        in_specs=[
            pl.BlockSpec((tm, d), lambda i: (i, 0)),
            _gain_spec(gi, d, 1),
            pl.BlockSpec((None, d, 2 * d), lambda i: (lj, 0, 0)),
            pl.BlockSpec((None, 1, 2 * d), lambda i: (lj, 0, 0)),
        ],
        out_specs=pl.BlockSpec((tm, d), lambda i: (i, 0)),
        out_shape=jax.ShapeDtypeStruct((t, d), F32),
        compiler_params=_cparams("parallel"),
    )(h, gains, w_pw1, b_pw1)


def _conv_b_kernel(h_ref, u_ref, prev_ref, wdw_ref, bdw_ref, lng_ref, lnb_ref, w2_ref, b2_ref,
                   gpost_ref, out_ref, e_ref, y_ref, wb_ref, *, width, from_prev_tile):
    g, ts, d = u_ref.shape
    lead = CONV_HALO - (width - 1)
    ext_rows = CONV_HALO + ts

    if from_prev_tile:
        first = pl.program_id(1) == 0

        @pl.when(first)
        def _():
            e_ref[0, :, :CONV_HALO, :] = jnp.zeros((g, CONV_HALO, d), F32)

        @pl.when(jnp.logical_not(first))
        def _():
            e_ref[0, :, :CONV_HALO, :] = prev_ref[...]
    else:
        e_ref[0, :, :lead, :] = jnp.zeros((g, lead, d), F32)
        e_ref[0, :, lead:CONV_HALO, :] = prev_ref[...]
    e_ref[0, :, CONV_HALO:, :] = u_ref[...]

    span = ext_rows - SUBLANES
    for r in range(1, SUBLANES):
        e_ref[r, :, :span, :] = e_ref[0, :, r:r + span, :]

    for k in range(width):
        wb_ref[k] = jnp.broadcast_to(wdw_ref[k:k + 1, :], (SUBLANES, d))

    sub = 2 if ts % (2 * SUBLANES) == 0 else 1
    chunk_rows = sub * SUBLANES
    chunks_per_seq = ts // chunk_rows

    def chunk(c, carry):
        gi = c // chunks_per_seq
        t0 = pl.multiple_of((c % chunks_per_seq) * chunk_rows, chunk_rows)
        accs = [jnp.broadcast_to(bdw_ref[...], (SUBLANES, d))] * sub
        for k in range(width):
            off = lead + k
            r = off % SUBLANES
            w = wb_ref[k]
            for a in range(sub):
                start = pl.multiple_of(t0 + (off - r) + a * SUBLANES, SUBLANES)
                accs[a] = accs[a] + e_ref[r, gi, pl.ds(start, SUBLANES), :] * w
        for a in range(sub):
            y_ref[gi, pl.ds(pl.multiple_of(t0 + a * SUBLANES, SUBLANES), SUBLANES), :] = accs[a]
        return carry

    lax.fori_loop(0, g * chunks_per_seq, chunk, 0)

    y = y_ref[...].reshape(g * ts, d)
    mu = jnp.mean(y, axis=-1, keepdims=True)
    var = jnp.mean(jnp.square(y - mu), axis=-1, keepdims=True)
    z = (y - mu) * lax.rsqrt(var + LN_EPS) * lng_ref[...] + lnb_ref[...]
    z = (z * jax.nn.sigmoid(z)).astype(BF16)
    m = _dot(z, w2_ref[...]) + b2_ref[...]
    h = h_ref[...].reshape(g * ts, d)
    out_ref[...] = (h + _rms(m, gpost_ref[...])).reshape(g, ts, d)


def _conv_b(h, u, prev, gains, gi_post, w_dw, b_dw, ln_g, ln_b, w_pw2, b_pw2, lj, group, ts):
    b, s, d = u.shape
    width = w_dw.shape[1]
    from_prev_tile = prev is None
    if from_prev_tile:
        assert ts % CONV_HALO == 0 and group == 1
        per = ts // CONV_HALO
        prev_arg = u
        prev_spec = pl.BlockSpec((group, CONV_HALO, d), lambda i, j: (i, jnp.maximum(j * per - 1, 0), 0))
    else:
        assert ts == s
        prev_arg = prev
        prev_spec = pl.BlockSpec((group, width - 1, d), lambda i, j: (i, 0, 0))
    vec = lambda: pl.BlockSpec((None, 1, d), lambda i, j: (lj, 0, 0))
    return pl.pallas_call(
        functools.partial(_conv_b_kernel, width=width, from_prev_tile=from_prev_tile),
        grid=(b // group, s // ts),
        in_specs=[
            pl.BlockSpec((group, ts, d), lambda i, j: (i, j, 0)),
            pl.BlockSpec((group, ts, d), lambda i, j: (i, j, 0)),
            prev_spec,
            pl.BlockSpec((None, width, d), lambda i, j: (lj, 0, 0)),
            vec(), vec(), vec(),
            pl.BlockSpec((None, d, d), lambda i, j: (lj, 0, 0)),
            vec(),
            _gain_spec(gi_post, d, 2),
        ],
        out_specs=pl.BlockSpec((group, ts, d), lambda i, j: (i, j, 0)),
        out_shape=jax.ShapeDtypeStruct((b, s, d), F32),
        scratch_shapes=[pltpu.VMEM((SUBLANES, group, CONV_HALO + ts, d), F32),
                        pltpu.VMEM((group, ts, d), F32),
                        pltpu.VMEM((width, SUBLANES, d), F32)],
        compiler_params=_cparams("parallel", "arbitrary"),
    )(h, u, prev_arg, w_dw, b_dw, ln_g, ln_b, w_pw2, b_pw2, gains)


def _mla_down(h_ref, g_ref, win_ref, qg_ref, kvg_ref, tab_ref, lat_ref, kpe_ref):
    xn = _rms(h_ref[...], g_ref[...]).astype(BF16)
    down = _dot(xn, win_ref[...])
    c_q = _rms(down[:, :Q_LORA], qg_ref[...])
    c_kv = _rms(down[:, Q_LORA:Q_LORA + KV_LORA], kvg_ref[...])
    base = Q_LORA + KV_LORA
    kpe = (down[:, base:base + LANES] * tab_ref[0] + down[:, base + LANES:base + 2 * LANES] * tab_ref[1])
    lat_ref[...] = c_kv
    kpe_ref[...] = kpe
    return c_q.astype(BF16), c_kv.astype(BF16), kpe


def _mla_pre_prompt_kernel(h_ref, g_ref, win_ref, qg_ref, kvg_ref, tab_ref, wqa_ref, wqb_ref, wk_ref, wv_ref,
                           lat_ref, kpe_ref, q_ref, k_ref, v_ref, *, scale):
    c_q, c_kv, kpe = _mla_down(h_ref, g_ref, win_ref, qg_ref, kvg_ref, tab_ref, lat_ref, kpe_ref)
    qa = _dot(c_q, wqa_ref[...])
    qb = _dot(c_q, wqb_ref[...])
    ka = _dot(c_kv, wk_ref[...])
    v_ref[...] = _dot(c_kv, wv_ref[...]).astype(BF16)
    cos = tab_ref[0] * scale
    sin = tab_ref[1] * scale
    for hh in range(q_ref.shape[0]):
        sl = slice(hh * LANES, (hh + 1) * LANES)
        q_ref[hh] = (qa[:, sl] * cos + qb[:, sl] * sin).astype(BF16)
        k_ref[hh] = (ka[:, sl] + kpe).astype(BF16)


def _mla_pre_sample_kernel(h_ref, g_ref, win_ref, qg_ref, kvg_ref, tab_ref, tabq_ref, wqn_ref, wqa_ref, wqb_ref,
                           wuk_ref, lat_ref, kpe_ref, qlat_ref, qpe_ref, *, scale):
    c_q, _, _ = _mla_down(h_ref, g_ref, win_ref, qg_ref, kvg_ref, tab_ref, lat_ref, kpe_ref)
    qn = _dot(c_q, wqn_ref[...]).astype(BF16)
    qpe = _dot(c_q, wqa_ref[...]) * tabq_ref[0] + _dot(c_q, wqb_ref[...]) * tabq_ref[1]
    qpe_ref[...] = qpe * scale
    nope = wuk_ref.shape[1]
    for hh in range(qlat_ref.shape[0]):
        qlat_ref[hh] = _dot(qn[:, hh * nope:(hh + 1) * nope], wuk_ref[hh]) * scale


def _full(a):
    nd = a.ndim
    return pl.BlockSpec(a.shape, lambda i: (0,) * nd)


def _mla_pre_prompt(h, gains, gi, w, tab, tm):
    t, d = h.shape
    heads = w["wqa"].shape[1] // LANES
    nper = tab.shape[1] // tm
    scale = float((QK_NOPE + QK_ROPE) ** -0.5)
    row = lambda width: pl.BlockSpec((tm, width), lambda i: (i, 0))
    return pl.pallas_call(
        functools.partial(_mla_pre_prompt_kernel, scale=scale),
        grid=(t // tm,),
        in_specs=[
            row(d), _gain_spec(gi, d, 1), _full(w["win"]), _full(w["qg"]), _full(w["kvg"]),
            pl.BlockSpec((2, tm, LANES), lambda i: (0, i % nper, 0)),
            _full(w["wqa"]), _full(w["wqb"]), _full(w["wk"]), _full(w["wv"]),
        ],
        out_specs=[
            row(KV_LORA), row(LANES),
            pl.BlockSpec((heads, tm, LANES), lambda i: (0, i, 0)),
            pl.BlockSpec((heads, tm, LANES), lambda i: (0, i, 0)),
            row(w["wv"].shape[1]),
        ],
        out_shape=[
            jax.ShapeDtypeStruct((t, KV_LORA), F32), jax.ShapeDtypeStruct((t, LANES), F32),
            jax.ShapeDtypeStruct((heads, t, LANES), BF16), jax.ShapeDtypeStruct((heads, t, LANES), BF16),
            jax.ShapeDtypeStruct((t, w["wv"].shape[1]), BF16),
        ],
        compiler_params=_cparams("parallel"),
    )(h, gains, w["win"], w["qg"], w["kvg"], tab, w["wqa"], w["wqb"], w["wk"], w["wv"])


def _mla_pre_sample(h, gains, gi, w, tab, tabq, tm):
    t, d = h.shape
    heads = w["wuk"].shape[0]
    pe_w = w["wqpa"].shape[1]
    scale = float((QK_NOPE + QK_ROPE) ** -0.5)
    row = lambda width: pl.BlockSpec((tm, width), lambda i: (i, 0))
    return pl.pallas_call(
        functools.partial(_mla_pre_sample_kernel, scale=scale),
        grid=(t // tm,),
        in_specs=[
            row(d), _gain_spec(gi, d, 1), _full(w["win"]), _full(w["qg"]), _full(w["kvg"]),
            _full(tab), _full(tabq),
            _full(w["wqn"]), _full(w["wqpa"]), _full(w["wqpb"]), _full(w["wuk"]),
        ],
        out_specs=[
            row(KV_LORA), row(LANES),
            pl.BlockSpec((heads, tm, KV_LORA), lambda i: (0, i, 0)),
            row(pe_w),
        ],
        out_shape=[
            jax.ShapeDtypeStruct((t, KV_LORA), F32), jax.ShapeDtypeStruct((t, LANES), F32),
            jax.ShapeDtypeStruct((heads, t, KV_LORA), F32), jax.ShapeDtypeStruct((t, pe_w), F32),
        ],
        compiler_params=_cparams("parallel"),
    )(h, gains, w["win"], w["qg"], w["kvg"], tab, tabq, w["wqn"], w["wqpa"], w["wqpb"], w["wuk"])


def _flash_kernel(q_ref, k_ref, v_ref, o_ref, *, vh):
    qi = pl.program_id(2)
    tq = q_ref.shape[1]
    nq = k_ref.shape[1] // tq

    def attend(n):
        past = n * tq
        row = lax.broadcasted_iota(jnp.int32, (tq, tq), 0)
        col = lax.broadcasted_iota(jnp.int32, (tq, tq), 1)
        res = []
        for hh in range(q_ref.shape[0]):
            q = q_ref[hh]
            s = [_dot_nt(q, k_ref[hh, c * tq:(c + 1) * tq, :]) for c in range(n + 1)]
            s[n] = jnp.where(col <= row, s[n], NEG_INF)
            m = functools.reduce(jnp.maximum, [jnp.max(sc, axis=-1, keepdims=True) for sc in s])
            l, o = None, None
            for c in range(n + 1):
                p = jnp.exp(s[c] - m)
                lc = jnp.sum(p, axis=-1, keepdims=True)
                oc = _dot(p.astype(BF16), v_ref[c * tq:(c + 1) * tq, :])
                l, o = (lc, oc) if l is None else (l + lc, o + oc)
            res.append(o / l)
        lane = lax.broadcasted_iota(jnp.int32, (tq, LANES), 1)
        o_ref[...] = jnp.where(lane < vh, res[0], res[1]).astype(BF16)

    for n in range(nq):
        pl.when(qi == n)(functools.partial(attend, n))


def _flash_prompt(q, k, v, batch, seq, tq):
    heads, t, _ = q.shape
    vh = v.shape[1] // heads
    per = LANES // vh
    assert per == 2
    nq = seq // tq
    return pl.pallas_call(
        functools.partial(_flash_kernel, vh=vh),
        grid=(batch, heads // per, nq),
        in_specs=[
            pl.BlockSpec((per, tq, LANES), lambda b, p, i: (p, b * nq + i, 0)),
            pl.BlockSpec((per, seq, LANES), lambda b, p, i: (p, b, 0)),
            pl.BlockSpec((seq, LANES), lambda b, p, i: (b, p)),
        ],
        out_specs=pl.BlockSpec((tq, LANES), lambda b, p, i: (b * nq + i, p)),
        out_shape=jax.ShapeDtypeStruct((t, heads * vh), BF16),
        compiler_params=_cparams("parallel", "parallel", "arbitrary"),
    )(q, k, v)


def _decode_kernel(pt_ref, qlat_ref, qpe_ref, cnew_ref, knew_ref, lat_hbm, krt_hbm, o_ref,
                   qpe_s, lat_buf, krt_buf, sem, *, pages):
    b = pl.program_id(0)
    slot = b % 2
    heads, nq, lat_w = qlat_ref.shape
    rows = heads * nq
    psz = lat_hbm.shape[1]
    rope = krt_hbm.shape[1]

    def page_copies(seq, to_slot, n):
        page = pt_ref[seq * pages + n]
        dst = pl.ds(pl.multiple_of(n * psz, psz), psz)
        return (pltpu.make_async_copy(lat_hbm.at[page], lat_buf.at[to_slot, dst, :], sem.at[0, to_slot]),
                pltpu.make_async_copy(krt_hbm.at[page], krt_buf.at[to_slot, :, dst], sem.at[1, to_slot]))

    def for_each_page(seq, to_slot, act):
        def body(n, carry):
            for cp in page_copies(seq, to_slot, n):
                act(cp)
            return carry
        lax.fori_loop(0, pages, body, 0)

    @pl.when(b == 0)
    def _():
        for_each_page(0, 0, lambda cp: cp.start())

    @pl.when(b + 1 < pl.num_programs(0))
    def _():
        for_each_page(b + 1, 1 - slot, lambda cp: cp.start())

    for hh in range(heads):
        qpe_s[hh * nq:(hh + 1) * nq, :] = qpe_ref[:, hh * rope:(hh + 1) * rope]
    q = qlat_ref[...].reshape(rows, lat_w).astype(BF16)
    qp = qpe_s[...].astype(BF16)

    cnew = jnp.concatenate([cnew_ref[...], jnp.zeros((psz - nq, lat_w), F32)], axis=0).astype(BF16)
    knew = jnp.concatenate([knew_ref[...], jnp.zeros((psz - nq, rope), F32)], axis=0).astype(BF16)
    s_own = _dot_nt(q, cnew) + _dot_nt(qp, knew)
    qpos = lax.broadcasted_iota(jnp.int32, s_own.shape, 0) % nq
    kpos = lax.broadcasted_iota(jnp.int32, s_own.shape, 1)
    s_own = jnp.where(kpos <= qpos, s_own, NEG_INF)

    for_each_page(b, slot, lambda cp: cp.wait())

    ck = DECODE_KEY_CHUNK
    lats, s = [cnew], [s_own]
    for c in range(pages * psz // ck):
        lat = lat_buf[slot, c * ck:(c + 1) * ck, :].astype(BF16)
        krt = krt_buf[slot, :, c * ck:(c + 1) * ck].astype(BF16)
        lats.append(lat)
        s.append(_dot_nt(q, lat) + _dot(qp, krt))
    m = functools.reduce(jnp.maximum, [jnp.max(sc, axis=-1, keepdims=True) for sc in s])
    l, o = None, None
    for sc, lat in zip(s, lats):
        p = jnp.exp(sc - m)
        lc = jnp.sum(p, axis=-1, keepdims=True)
        oc = _dot(p.astype(BF16), lat)
        l, o = (lc, oc) if l is None else (l + lc, o + oc)
    o_ref[...] = o / l


def _decode_sample(page_table, qlat, qpe, cnew, knew, lat_pool, krt_pool):
    heads, t, lat_w = qlat.shape
    nb, pages = page_table.shape
    nq = t // nb
    psz = lat_pool.shape[1]
    rope = krt_pool.shape[1]
    rows = heads * nq
    past = pages * psz
    assert past % DECODE_KEY_CHUNK == 0
    grid_spec = pltpu.PrefetchScalarGridSpec(
        num_scalar_prefetch=1,
        grid=(nb,),
        in_specs=[
            pl.BlockSpec((heads, nq, lat_w), lambda b, pt_ref: (0, b, 0)),
            pl.BlockSpec((nq, qpe.shape[1]), lambda b, pt_ref: (b, 0)),
            pl.BlockSpec((nq, lat_w), lambda b, pt_ref: (b, 0)),
            pl.BlockSpec((nq, rope), lambda b, pt_ref: (b, 0)),
            pl.BlockSpec(memory_space=pl.ANY),
            pl.BlockSpec(memory_space=pl.ANY),
        ],
        out_specs=pl.BlockSpec((None, rows, lat_w), lambda b, pt_ref: (b, 0, 0)),
        scratch_shapes=[pltpu.VMEM((rows, rope), F32),
                        pltpu.VMEM((2, past, lat_w), F32), pltpu.VMEM((2, rope, past), F32),
                        pltpu.SemaphoreType.DMA((2, 2))],
    )
    return pl.pallas_call(
        functools.partial(_decode_kernel, pages=pages),
        grid_spec=grid_spec,
        out_shape=jax.ShapeDtypeStruct((nb, rows, lat_w), F32),
        compiler_params=_cparams("arbitrary"),
    )(page_table.reshape(-1), qlat, qpe, cnew, knew, lat_pool, krt_pool)


def _mla_post_prompt_kernel(o_ref, h_ref, wo_ref, g_ref, out_ref):
    out_ref[...] = h_ref[...] + _rms(_dot(o_ref[...], wo_ref[...]), g_ref[...])


def _mla_post_prompt(o, h, gains, gi, w_o):
    t, d = h.shape
    tm = _row_tile(t)
    return pl.pallas_call(
        _mla_post_prompt_kernel,
        grid=(t // tm,),
        in_specs=[
            pl.BlockSpec((tm, o.shape[1]), lambda i: (i, 0)),
            pl.BlockSpec((tm, d), lambda i: (i, 0)),
            _full(w_o), _gain_spec(gi, d, 1),
        ],
        out_specs=pl.BlockSpec((tm, d), lambda i: (i, 0)),
        out_shape=jax.ShapeDtypeStruct((t, d), F32),
        compiler_params=_cparams("parallel"),
    )(o, h, w_o, gains)


def _mla_post_sample_kernel(olat_ref, h_ref, wuv_ref, wo_ref, g_ref, out_ref):
    gb, rows, lat_w = olat_ref.shape
    heads = wuv_ref.shape[0]
    nq = rows // heads
    acc = None
    for hh in range(heads):
        x = olat_ref[:, hh * nq:(hh + 1) * nq, :].reshape(gb * nq, lat_w).astype(BF16)
        oh = _dot(x, wuv_ref[hh]).astype(BF16)
        part = _dot(oh, wo_ref[hh])
        acc = part if acc is None else acc + part
    out_ref[...] = h_ref[...] + _rms(acc, g_ref[...])


def _mla_post_sample(olat, h, gains, gi, w_uv, w_o, gb):
    nb, rows, lat_w = olat.shape
    t, d = h.shape
    nq = t // nb
    return pl.pallas_call(
        _mla_post_sample_kernel,
        grid=(nb // gb,),
        in_specs=[
            pl.BlockSpec((gb, rows, lat_w), lambda i: (i, 0, 0)),
            pl.BlockSpec((gb * nq, d), lambda i: (i, 0)),
            _full(w_uv), _full(w_o), _gain_spec(gi, d, 1),
        ],
        out_specs=pl.BlockSpec((gb * nq, d), lambda i: (i, 0)),
        out_shape=jax.ShapeDtypeStruct((t, d), F32),
        compiler_params=_cparams("parallel"),
    )(olat, h, w_uv, w_o, gains)


def _rot_cols(w):
    half = w.shape[-1] // 2
    return jnp.concatenate([-w[..., half:], w[..., :half]], axis=-1)


def _rope_tables(pos, reps):
    inv_freq = ROPE_THETA ** (-jnp.arange(0, QK_ROPE, 2, dtype=F32) / QK_ROPE)
    ang = pos.astype(F32)[:, None] * inv_freq[None, :]
    cos, sin = jnp.cos(ang), jnp.sin(ang)
    p = pos.shape[0]
    pad = LANES - QK_NOPE - QK_ROPE
    cos_g = jnp.concatenate([jnp.ones((p, QK_NOPE), F32), cos, cos, jnp.ones((p, pad), F32)], axis=1)
    sin_g = jnp.concatenate([jnp.zeros((p, QK_NOPE), F32), sin, sin, jnp.zeros((p, pad), F32)], axis=1)
    tab = jnp.stack([cos_g, sin_g])
    cos_q = jnp.tile(jnp.concatenate([cos, cos], axis=1), (1, reps))
    sin_q = jnp.tile(jnp.concatenate([sin, sin], axis=1), (1, reps))
    return tab, jnp.stack([cos_q, sin_q])


def _mla_weights(w_in, q_g, kv_g, w_uq, w_ukv, w_o):
    d = w_in.shape[0]
    heads = w_uq.shape[1]
    pad = LANES - QK_NOPE - QK_ROPE
    wq, wkv, wpe = w_in[:, :Q_LORA], w_in[:, Q_LORA:Q_LORA + KV_LORA], w_in[:, Q_LORA + KV_LORA:]
    zn, zp = jnp.zeros((d, QK_NOPE), F32), jnp.zeros((d, pad), F32)
    win = jnp.concatenate([wq, wkv, zn, wpe, zp, zn, _rot_cols(wpe), zp], axis=1).astype(BF16)
    w_nope, w_pe = w_uq[..., :QK_NOPE], w_uq[..., QK_NOPE:]
    w_pe_rot = _rot_cols(w_pe)
    w_uk, w_uv = w_ukv[..., :QK_NOPE], w_ukv[..., QK_NOPE:]
    vh = w_uv.shape[-1]
    zq_n = jnp.zeros((Q_LORA, heads, QK_NOPE), F32)
    zq_p = jnp.zeros((Q_LORA, heads, pad), F32)
    zk = jnp.zeros((KV_LORA, heads, LANES - QK_NOPE), F32)
    return dict(
        win=win, qg=q_g[None, :], kvg=kv_g[None, :],
        wqa=jnp.concatenate([w_nope, w_pe, zq_p], axis=-1).reshape(Q_LORA, heads * LANES).astype(BF16),
        wqb=jnp.concatenate([zq_n, w_pe_rot, zq_p], axis=-1).reshape(Q_LORA, heads * LANES).astype(BF16),
        wk=jnp.concatenate([w_uk, zk], axis=-1).reshape(KV_LORA, heads * LANES).astype(BF16),
        wv=w_uv.reshape(KV_LORA, heads * vh).astype(BF16),
        wo=w_o.astype(BF16),
        wqn=w_nope.reshape(Q_LORA, heads * QK_NOPE).astype(BF16),
        wqpa=w_pe.reshape(Q_LORA, heads * QK_ROPE).astype(BF16),
        wqpb=w_pe_rot.reshape(Q_LORA, heads * QK_ROPE).astype(BF16),
        wuk=jnp.transpose(w_uk, (1, 2, 0)).astype(BF16),
        wuv=jnp.transpose(w_uv, (1, 0, 2)).astype(BF16),
        wo_h=w_o.reshape(heads, vh, w_o.shape[1]).astype(BF16),
    )


def _trunk(x, is_prompt, W, conv_prevs, mla_pasts, page_table, mem_k, mem_v, mem_rows_per_layer):
    b, s, d = x.shape
    t = b * s
    h = x.reshape(t, d)
    depth = W["ffn_w_out"].shape[0] // 2
    gains = W["gains"]
    heads = W["mla"][0]["wuk"].shape[0]
    tm = _row_tile(t)
    if is_prompt:
        seq_tile = _row_tile(s)
        tab, _ = _rope_tables(jnp.arange(s, dtype=jnp.int32), heads)
        xa_group, xa_ts = 1, seq_tile
    else:
        past_len = page_table.shape[1] * mla_pasts[0][0].shape[1]
        pos = past_len + jnp.arange(s, dtype=jnp.int32)
        tab, tabq = _rope_tables(jnp.tile(pos, tm // s), heads)
        xa_group, xa_ts = min(XATTN_GROUP, b), s
    conv_new, lat_new, rope_new = [], [], []
    for i in range(depth):
        g0 = i * 8
        h = _ffn(h, gains, g0 + 0, g0 + 1, W["ffn_w_in"], W["ffn_w_out"], 2 * i)
        j = i // 2
        if i % 2 == 0:
            u = _conv_a(h, gains, g0 + 2, W["conv_w_pw1"], W["conv_b_pw1"], j)
            u3, h3 = u.reshape(b, s, d), h.reshape(b, s, d)
            cw = W["conv_w_dw"].shape[1]
            if is_prompt:
                hn = _conv_b(h3, u3, None, gains, g0 + 3, W["conv_w_dw"], W["conv_b_dw"], W["conv_ln_g"],
                             W["conv_ln_b"], W["conv_w_pw2"], W["conv_b_pw2"], j, 1, seq_tile)
                conv_new.append(u3[:, s - (cw - 1):])
            else:
                prev = conv_prevs[j]
                hn = _conv_b(h3, u3, prev, gains, g0 + 3, W["conv_w_dw"], W["conv_b_dw"], W["conv_ln_g"],
                             W["conv_ln_b"], W["conv_w_pw2"], W["conv_b_pw2"], j, min(CONV_GROUP, b), s)
                conv_new.append(jnp.concatenate([prev, u3], axis=1)[:, -(cw - 1):])
            h = hn.reshape(t, d)
        else:
            w = W["mla"][j]
            if is_prompt:
                lat, kpe, q, k, v = _mla_pre_prompt(h, gains, g0 + 2, w, tab, seq_tile)
                o = _flash_prompt(q, k, v, b, s, seq_tile)
                h = _mla_post_prompt(o, h, gains, g0 + 3, w["wo"])
            else:
                lat, kpe, qlat, qpe = _mla_pre_sample(h, gains, g0 + 2, w, tab, tabq, tm)
                knew = kpe[:, QK_NOPE:QK_NOPE + QK_ROPE]
                lat_pool, kr_pool = mla_pasts[j]
                olat = _decode_sample(page_table, qlat, qpe, lat, knew, lat_pool, jnp.swapaxes(kr_pool, 1, 2))
                h = _mla_post_sample(olat, h, gains, g0 + 3, w["wuv"], w["wo_h"], min(b, 64))
            lat_new.append(lat.reshape(b, s, KV_LORA))
            rope_new.append(kpe[:, QK_NOPE:QK_NOPE + QK_ROPE].reshape(b, s, QK_ROPE))
        h = _xattn(h.reshape(b, s, d), mem_k, mem_v, i * mem_rows_per_layer, gains, g0 + 4, g0 + 5,
                   W["xa_w_q"], W["xa_w_o"], i, xa_group, xa_ts).reshape(t, d)
        h = _ffn(h, gains, g0 + 6, g0 + 7, W["ffn_w_in"], W["ffn_w_out"], 2 * i + 1)
    return h.reshape(b, s, d), conv_new, lat_new, rope_new


def kernel(x_prompt, x_sample, state_conv_l0, state_conv_l2, cache_mla_latent_l1, cache_mla_krope_l1, cache_mla_latent_l3, cache_mla_krope_l3, cache_mem_k, cache_mem_v, page_table, mem_prompt, norm_gain, ffn_w_in, ffn_w_out, conv_w_pw1, conv_b_pw1, conv_w_dw, conv_b_dw, conv_ln_g, conv_ln_b, conv_w_pw2, conv_b_pw2, mla_w_in, mla_q_norm, mla_kv_norm, mla_w_uq, mla_w_ukv, mla_w_o, xa_mem_norm, xa_w_q, xa_w_kv, xa_w_o):
    depth, _, d = norm_gain.shape
    dff = ffn_w_out.shape[2]
    W = dict(
        gains=norm_gain.reshape(depth * 8, 1, d),
        ffn_w_in=_ffn_pack_w_in(ffn_w_in.reshape(depth * 2, d, 2 * dff), dff).astype(BF16),
        ffn_w_out=ffn_w_out.reshape(depth * 2, dff, d).astype(BF16),
        conv_w_pw1=conv_w_pw1.astype(BF16), conv_b_pw1=conv_b_pw1[:, None, :],
        conv_w_dw=conv_w_dw, conv_b_dw=conv_b_dw[:, None, :],
        conv_ln_g=conv_ln_g[:, None, :], conv_ln_b=conv_ln_b[:, None, :],
        conv_w_pw2=conv_w_pw2.astype(BF16), conv_b_pw2=conv_b_pw2[:, None, :],
        xa_w_q=xa_w_q.astype(BF16), xa_w_o=xa_w_o.astype(BF16),
        mla=[_mla_weights(mla_w_in[j], mla_q_norm[j], mla_kv_norm[j], mla_w_uq[j], mla_w_ukv[j], mla_w_o[j])
             for j in range(mla_w_in.shape[0])],
    )

    bp, n_mem, _ = mem_prompt.shape
    mk, mv = _memkv(mem_prompt.reshape(bp * n_mem, d), xa_mem_norm[:, None, :], xa_w_kv.astype(BF16))
    xw = mk.shape[-1]
    y_prompt, p_conv, p_lat, p_rope = _trunk(
        x_prompt, True, W, None, None, None,
        mk.reshape(depth * bp, n_mem, xw), mv.reshape(depth * bp, n_mem, xw), bp)
    xa_heads = xw // XA_HEAD_DIM
    mem_k_prompt = mk.reshape(depth, bp, n_mem, xa_heads, XA_HEAD_DIM)
    mem_v_prompt = mv.reshape(depth, bp, n_mem, xa_heads, XA_HEAD_DIM)

    bs = x_sample.shape[0]
    y_sample, s_conv, s_lat, s_rope = _trunk(
        x_sample, False, W, [state_conv_l0, state_conv_l2],
        [(cache_mla_latent_l1, cache_mla_krope_l1), (cache_mla_latent_l3, cache_mla_krope_l3)], page_table,
        cache_mem_k.reshape(depth * bs, n_mem * xa_heads, XA_HEAD_DIM),
        cache_mem_v.reshape(depth * bs, n_mem * xa_heads, XA_HEAD_DIM), bs)

    return (y_prompt, y_sample,
            p_conv[0], p_conv[1], p_lat[0], p_rope[0], p_lat[1], p_rope[1],
            mem_k_prompt, mem_v_prompt,
            s_conv[0], s_conv[1], s_lat[0], s_rope[0], s_lat[1], s_rope[1])
```

```python
import functools

import jax
import jax.numpy as jnp
from jax import lax
from jax.experimental import pallas as pl
from jax.experimental.pallas import tpu as pltpu

F32 = jnp.float32
BF16 = jnp.bfloat16

RMS_EPS = 1e-6
LN_EPS = 1e-5
ROPE_THETA = 10000.0
FFN_RESID = 0.5
NEG_INF = -1e30

Q_LORA = 384
KV_LORA = 256
QK_NOPE = 64
QK_ROPE = 32
XA_HEAD_DIM = 128

LANES = 128
SUBLANES = 8
VMEM_LIMIT_BYTES = 52 * 1024 * 1024

ROW_TILE = 512
CONV_HALO = 32
FLASH_HEADS = 8
DECODE_KEY_CHUNK = 1024
XATTN_GROUP = 8
CONV_GROUP = 8


def _cparams(*sem):
    return pltpu.CompilerParams(dimension_semantics=sem, vmem_limit_bytes=VMEM_LIMIT_BYTES)


def _row_tile(t):
    return ROW_TILE if t % ROW_TILE == 0 else t


def _rms(x, g):
    return x * lax.rsqrt(jnp.mean(x * x, axis=-1, keepdims=True) + RMS_EPS) * g


def _dot(a, b):
    return jnp.dot(a, b, preferred_element_type=F32)


def _dot_nt(a, b):
    return lax.dot_general(a, b, (((1,), (1,)), ((), ())), preferred_element_type=F32)


def _dot_tn(a, b):
    return lax.dot_general(a, b, (((0,), (0,)), ((), ())), preferred_element_type=F32)


def _gain_spec(idx, d, ngrid):
    if ngrid == 1:
        return pl.BlockSpec((None, 1, d), lambda i: (idx, 0, 0))
    return pl.BlockSpec((None, 1, d), lambda i, j: (idx, 0, 0))


def _ffn_kernel(h_ref, gpre_ref, gpost_ref, wg_ref, wu_ref, wo_ref, out_ref, xn_ref, acc_ref):
    j = pl.program_id(1)

    @pl.when(j == 0)
    def _():
        xn_ref[...] = _rms(h_ref[...], gpre_ref[...]).astype(BF16)
        acc_ref[...] = jnp.zeros_like(acc_ref)

    tf = wo_ref.shape[0]
    gu = _dot(xn_ref[...], jnp.concatenate([wg_ref[...], wu_ref[...]], axis=1))
    gate, up = gu[:, :tf], gu[:, tf:]
    a = (gate * jax.nn.sigmoid(gate) * up).astype(BF16)
    acc_ref[...] += _dot(a, wo_ref[...])

    @pl.when(j == pl.num_programs(1) - 1)
    def _():
        out_ref[...] = h_ref[...] + FFN_RESID * _rms(acc_ref[...], gpost_ref[...])


def _ffn_chunk(f):
    return f // 2 if (f // 2) % LANES == 0 else f


def _ffn(h, gains, gi_pre, gi_post, w_in, w_out, li):
    t, d = h.shape
    f = w_out.shape[1]
    tm = _row_tile(t)
    tf = _ffn_chunk(f)
    nf = f // tf
    return pl.pallas_call(
        _ffn_kernel,
        grid=(t // tm, nf),
        in_specs=[
            pl.BlockSpec((tm, d), lambda i, j: (i, 0)),
            _gain_spec(gi_pre, d, 2),
            _gain_spec(gi_post, d, 2),
            pl.BlockSpec((None, d, tf), lambda i, j: (li, 0, j)),
            pl.BlockSpec((None, d, tf), lambda i, j: (li, 0, j + nf)),
            pl.BlockSpec((None, tf, d), lambda i, j: (li, j, 0)),
        ],
        out_specs=pl.BlockSpec((tm, d), lambda i, j: (i, 0)),
        out_shape=jax.ShapeDtypeStruct((t, d), F32),
        scratch_shapes=[pltpu.VMEM((tm, d), BF16), pltpu.VMEM((tm, d), F32)],
        compiler_params=_cparams("parallel", "arbitrary"),
    )(h, gains, gains, w_in, w_in, w_out)


def _memkv_kernel(mem_ref, g_ref, w_ref, k_ref, v_ref):
    xn = _rms(mem_ref[...], g_ref[...]).astype(BF16)
    kv = _dot(xn, w_ref[...])
    xw = k_ref.shape[-1]
    k_ref[...] = kv[:, :xw]
    v_ref[...] = kv[:, xw:]


def _memkv(mem, gains, w_kv):
    m, d = mem.shape
    depth, _, xw2 = w_kv.shape
    xw = xw2 // 2
    tm = _row_tile(m)
    out = jax.ShapeDtypeStruct((depth, m, xw), F32)
    return pl.pallas_call(
        _memkv_kernel,
        grid=(depth, m // tm),
        in_specs=[
            pl.BlockSpec((tm, d), lambda l, i: (i, 0)),
            pl.BlockSpec((None, 1, d), lambda l, i: (l, 0, 0)),
            pl.BlockSpec((None, d, xw2), lambda l, i: (l, 0, 0)),
        ],
        out_specs=[pl.BlockSpec((None, tm, xw), lambda l, i: (l, i, 0))] * 2,
        out_shape=[out, out],
        compiler_params=_cparams("parallel", "parallel"),
    )(mem, gains, w_kv)


def _softmax_pv(s, v):
    e = jnp.exp(s - jnp.max(s, axis=-1, keepdims=True))
    l = jnp.sum(e, axis=-1, keepdims=True)
    return jnp.einsum("gqk,gkd->gqd", e.astype(BF16), v, preferred_element_type=F32) / l


def _xattn_kernel(h_ref, k_ref, v_ref, gpre_ref, gpost_ref, wq_ref, wo_ref, out_ref, *, heads):
    g, ts, d = h_ref.shape
    xw = wq_ref.shape[-1]
    hd = xw // heads
    h = h_ref[...].reshape(g * ts, d)
    xn = _rms(h, gpre_ref[...]).astype(BF16)
    q = (_dot(xn, wq_ref[...]) * (hd ** -0.5)).reshape(g, ts, xw)
    k = k_ref[...].astype(BF16)
    v = v_ref[...].astype(BF16)
    if k_ref.shape[-1] == xw:
        q = q.astype(BF16)
        outs = []
        for hh in range(heads):
            sl = slice(hh * hd, (hh + 1) * hd)
            s = jnp.einsum("gqd,gkd->gqk", q[:, :, sl], k[:, :, sl], preferred_element_type=F32)
            outs.append(_softmax_pv(s, v[:, :, sl]))
    else:
        qs = jnp.concatenate([q[:, :, hh * hd:(hh + 1) * hd] for hh in range(heads)], axis=1).astype(BF16)
        s = jnp.einsum("gqd,gkd->gqk", qs, k, preferred_element_type=F32)
        q_head = lax.broadcasted_iota(jnp.int32, s.shape, 1) // ts
        k_head = lax.broadcasted_iota(jnp.int32, s.shape, 2) % heads
        o = _softmax_pv(jnp.where(q_head == k_head, s, NEG_INF), v)
        outs = [o[:, hh * ts:(hh + 1) * ts, :] for hh in range(heads)]
    o = jnp.concatenate(outs, axis=-1).reshape(g * ts, xw).astype(BF16)
    c = _dot(o, wo_ref[...])
    out_ref[...] = (h + _rms(c, gpost_ref[...])).reshape(g, ts, d)


def _xattn(h, k, v, kv_base, gains, gi_pre, gi_post, w_q, w_o, li, group, ts):
    b, s, d = h.shape
    _, m, kw = k.shape
    xw = w_q.shape[-1]
    heads = xw // XA_HEAD_DIM
    kvb = kv_base // group
    return pl.pallas_call(
        functools.partial(_xattn_kernel, heads=heads),
        grid=(b // group, s // ts),
        in_specs=[
            pl.BlockSpec((group, ts, d), lambda i, j: (i, j, 0)),
            pl.BlockSpec((group, m, kw), lambda i, j: (kvb + i, 0, 0)),
            pl.BlockSpec((group, m, kw), lambda i, j: (kvb + i, 0, 0)),
            _gain_spec(gi_pre, d, 2),
            _gain_spec(gi_post, d, 2),
            pl.BlockSpec((None, d, xw), lambda i, j: (li, 0, 0)),
            pl.BlockSpec((None, xw, d), lambda i, j: (li, 0, 0)),
        ],
        out_specs=pl.BlockSpec((group, ts, d), lambda i, j: (i, j, 0)),
        out_shape=jax.ShapeDtypeStruct((b, s, d), F32),
        compiler_params=_cparams("parallel", "parallel"),
    )(h, k, v, gains, gains, w_q, w_o)


def _conv_a_kernel(h_ref, g_ref, w_ref, b_ref, u_ref):
    d = u_ref.shape[-1]
    xn = _rms(h_ref[...], g_ref[...]).astype(BF16)
    y = _dot(xn, w_ref[...]) + b_ref[...]
    u_ref[...] = y[:, :d] * jax.nn.sigmoid(y[:, d:])


def _conv_a(h, gains, gi, w_pw1, b_pw1, lj):
    t, d = h.shape
    tm = _row_tile(t)
    return pl.pallas_call(
        _conv_a_kernel,
        grid=(t // tm,),
        in_specs=[
            pl.BlockSpec((tm, d), lambda i: (i, 0)),
            _gain_spec(gi, d, 1),
            pl.BlockSpec((None, d, 2 * d), lambda i: (lj, 0, 0)),
            pl.BlockSpec((None, 1, 2 * d), lambda i: (lj, 0, 0)),
        ],
        out_specs=pl.BlockSpec((tm, d), lambda i: (i, 0)),
        out_shape=jax.ShapeDtypeStruct((t, d), F32),
        compiler_params=_cparams("parallel"),
    )(h, gains, w_pw1, b_pw1)


def _conv_b_kernel(h_ref, u_ref, prev_ref, wdw_ref, bdw_ref, lng_ref, lnb_ref, w2_ref, b2_ref,
                   gpost_ref, out_ref, e_ref, y_ref, wb_ref, *, width, from_prev_tile):
    g, ts, d = u_ref.shape
    lead = CONV_HALO - (width - 1)
    ext_rows = CONV_HALO + ts

    if from_prev_tile:
        first = pl.program_id(1) == 0

        @pl.when(first)
        def _():
            e_ref[0, :, :CONV_HALO, :] = jnp.zeros((g, CONV_HALO, d), F32)

        @pl.when(jnp.logical_not(first))
        def _():
            e_ref[0, :, :CONV_HALO, :] = prev_ref[...]
    else:
        e_ref[0, :, :lead, :] = jnp.zeros((g, lead, d), F32)
        e_ref[0, :, lead:CONV_HALO, :] = prev_ref[...]
    e_ref[0, :, CONV_HALO:, :] = u_ref[...]

    span = ext_rows - SUBLANES
    for r in range(1, SUBLANES):
        e_ref[r, :, :span, :] = e_ref[0, :, r:r + span, :]

    for k in range(width):
        wb_ref[k] = jnp.broadcast_to(wdw_ref[k:k + 1, :], (SUBLANES, d))

    sub = 2 if ts % (2 * SUBLANES) == 0 else 1
    chunk_rows = sub * SUBLANES
    chunks_per_seq = ts // chunk_rows

    def chunk(c, carry):
        gi = c // chunks_per_seq
        t0 = pl.multiple_of((c % chunks_per_seq) * chunk_rows, chunk_rows)
        accs = [jnp.broadcast_to(bdw_ref[...], (SUBLANES, d))] * sub
        for k in range(width):
            off = lead + k
            r = off % SUBLANES
            w = wb_ref[k]
            for a in range(sub):
                start = pl.multiple_of(t0 + (off - r) + a * SUBLANES, SUBLANES)
                accs[a] = accs[a] + e_ref[r, gi, pl.ds(start, SUBLANES), :] * w
        for a in range(sub):
            y_ref[gi, pl.ds(pl.multiple_of(t0 + a * SUBLANES, SUBLANES), SUBLANES), :] = accs[a]
        return carry

    lax.fori_loop(0, g * chunks_per_seq, chunk, 0)

    y = y_ref[...].reshape(g * ts, d)
    mu = jnp.mean(y, axis=-1, keepdims=True)
    var = jnp.mean(jnp.square(y - mu), axis=-1, keepdims=True)
    z = (y - mu) * lax.rsqrt(var + LN_EPS) * lng_ref[...] + lnb_ref[...]
    z = (z * jax.nn.sigmoid(z)).astype(BF16)
    m = _dot(z, w2_ref[...]) + b2_ref[...]
    h = h_ref[...].reshape(g * ts, d)
    out_ref[...] = (h + _rms(m, gpost_ref[...])).reshape(g, ts, d)


def _conv_b(h, u, prev, gains, gi_post, w_dw, b_dw, ln_g, ln_b, w_pw2, b_pw2, lj, group, ts):
    b, s, d = u.shape
    width = w_dw.shape[1]
    from_prev_tile = prev is None
    if from_prev_tile:
        assert ts % CONV_HALO == 0 and group == 1
        per = ts // CONV_HALO
        prev_arg = u
        prev_spec = pl.BlockSpec((group, CONV_HALO, d), lambda i, j: (i, jnp.maximum(j * per - 1, 0), 0))
    else:
        assert ts == s
        prev_arg = prev
        prev_spec = pl.BlockSpec((group, width - 1, d), lambda i, j: (i, 0, 0))
    vec = lambda: pl.BlockSpec((None, 1, d), lambda i, j: (lj, 0, 0))
    return pl.pallas_call(
        functools.partial(_conv_b_kernel, width=width, from_prev_tile=from_prev_tile),
        grid=(b // group, s // ts),
        in_specs=[
            pl.BlockSpec((group, ts, d), lambda i, j: (i, j, 0)),
            pl.BlockSpec((group, ts, d), lambda i, j: (i, j, 0)),
            prev_spec,
            pl.BlockSpec((None, width, d), lambda i, j: (lj, 0, 0)),
            vec(), vec(), vec(),
            pl.BlockSpec((None, d, d), lambda i, j: (lj, 0, 0)),
            vec(),
            _gain_spec(gi_post, d, 2),
        ],
        out_specs=pl.BlockSpec((group, ts, d), lambda i, j: (i, j, 0)),
        out_shape=jax.ShapeDtypeStruct((b, s, d), F32),
        scratch_shapes=[pltpu.VMEM((SUBLANES, group, CONV_HALO + ts, d), F32),
                        pltpu.VMEM((group, ts, d), F32),
                        pltpu.VMEM((width, SUBLANES, d), F32)],
        compiler_params=_cparams("parallel", "arbitrary"),
    )(h, u, prev_arg, w_dw, b_dw, ln_g, ln_b, w_pw2, b_pw2, gains)


def _mla_down(h_ref, g_ref, win_ref, qg_ref, kvg_ref, tab_ref, lat_ref, kpe_ref):
    xn = _rms(h_ref[...], g_ref[...]).astype(BF16)
    down = _dot(xn, win_ref[...])
    c_q = _rms(down[:, :Q_LORA], qg_ref[...])
    c_kv = _rms(down[:, Q_LORA:Q_LORA + KV_LORA], kvg_ref[...])
    base = Q_LORA + KV_LORA
    kpe = (down[:, base:base + LANES] * tab_ref[0] + down[:, base + LANES:base + 2 * LANES] * tab_ref[1])
    lat_ref[...] = c_kv
    kpe_ref[...] = kpe
    return c_q.astype(BF16), c_kv.astype(BF16), kpe


def _mla_pre_prompt_kernel(h_ref, g_ref, win_ref, qg_ref, kvg_ref, tab_ref, wqa_ref, wqb_ref, wk_ref, wv_ref,
                           lat_ref, kpe_ref, q_ref, k_ref, v_ref, *, scale):
    c_q, c_kv, kpe = _mla_down(h_ref, g_ref, win_ref, qg_ref, kvg_ref, tab_ref, lat_ref, kpe_ref)
    qa = _dot(c_q, wqa_ref[...])
    qb = _dot(c_q, wqb_ref[...])
    ka = _dot(c_kv, wk_ref[...])
    v_ref[...] = _dot(c_kv, wv_ref[...]).astype(BF16)
    cos = tab_ref[0] * scale
    sin = tab_ref[1] * scale
    for hh in range(q_ref.shape[0]):
        sl = slice(hh * LANES, (hh + 1) * LANES)
        q_ref[hh] = (qa[:, sl] * cos + qb[:, sl] * sin).astype(BF16)
        k_ref[hh] = (ka[:, sl] + kpe).astype(BF16)


def _mla_pre_sample_kernel(h_ref, g_ref, win_ref, qg_ref, kvg_ref, tab_ref, tabq_ref, wqn_ref, wqa_ref, wqb_ref,
                           wuk_ref, lat_ref, kpe_ref, qlat_ref, qpe_ref, *, scale):
    c_q, _, _ = _mla_down(h_ref, g_ref, win_ref, qg_ref, kvg_ref, tab_ref, lat_ref, kpe_ref)
    qn = _dot(c_q, wqn_ref[...]).astype(BF16)
    qpe = _dot(c_q, wqa_ref[...]) * tabq_ref[0] + _dot(c_q, wqb_ref[...]) * tabq_ref[1]
    qpe_ref[...] = qpe * scale
    nope = wuk_ref.shape[1]
    for hh in range(qlat_ref.shape[0]):
        qlat_ref[hh] = _dot(qn[:, hh * nope:(hh + 1) * nope], wuk_ref[hh]) * scale


def _full(a):
    nd = a.ndim
    return pl.BlockSpec(a.shape, lambda i: (0,) * nd)


def _mla_pre_prompt(h, gains, gi, w, tab, tm):
    t, d = h.shape
    heads = w["wqa"].shape[1] // LANES
    nper = tab.shape[1] // tm
    scale = float((QK_NOPE + QK_ROPE) ** -0.5)
    row = lambda width: pl.BlockSpec((tm, width), lambda i: (i, 0))
    return pl.pallas_call(
        functools.partial(_mla_pre_prompt_kernel, scale=scale),
        grid=(t // tm,),
        in_specs=[
            row(d), _gain_spec(gi, d, 1), _full(w["win"]), _full(w["qg"]), _full(w["kvg"]),
            pl.BlockSpec((2, tm, LANES), lambda i: (0, i % nper, 0)),
            _full(w["wqa"]), _full(w["wqb"]), _full(w["wk"]), _full(w["wv"]),
        ],
        out_specs=[
            row(KV_LORA), row(LANES),
            pl.BlockSpec((heads, tm, LANES), lambda i: (0, i, 0)),
            pl.BlockSpec((heads, tm, LANES), lambda i: (0, i, 0)),
            row(w["wv"].shape[1]),
        ],
        out_shape=[
            jax.ShapeDtypeStruct((t, KV_LORA), F32), jax.ShapeDtypeStruct((t, LANES), F32),
            jax.ShapeDtypeStruct((heads, t, LANES), BF16), jax.ShapeDtypeStruct((heads, t, LANES), BF16),
            jax.ShapeDtypeStruct((t, w["wv"].shape[1]), BF16),
        ],
        compiler_params=_cparams("parallel"),
    )(h, gains, w["win"], w["qg"], w["kvg"], tab, w["wqa"], w["wqb"], w["wk"], w["wv"])


def _mla_pre_sample(h, gains, gi, w, tab, tabq, tm):
    t, d = h.shape
    heads = w["wuk"].shape[0]
    pe_w = w["wqpa"].shape[1]
    scale = float((QK_NOPE + QK_ROPE) ** -0.5)
    row = lambda width: pl.BlockSpec((tm, width), lambda i: (i, 0))
    return pl.pallas_call(
        functools.partial(_mla_pre_sample_kernel, scale=scale),
        grid=(t // tm,),
        in_specs=[
            row(d), _gain_spec(gi, d, 1), _full(w["win"]), _full(w["qg"]), _full(w["kvg"]),
            _full(tab), _full(tabq),
            _full(w["wqn"]), _full(w["wqpa"]), _full(w["wqpb"]), _full(w["wuk"]),
        ],
        out_specs=[
            row(KV_LORA), row(LANES),
            pl.BlockSpec((heads, tm, KV_LORA), lambda i: (0, i, 0)),
            row(pe_w),
        ],
        out_shape=[
            jax.ShapeDtypeStruct((t, KV_LORA), F32), jax.ShapeDtypeStruct((t, LANES), F32),
            jax.ShapeDtypeStruct((heads, t, KV_LORA), F32), jax.ShapeDtypeStruct((t, pe_w), F32),
        ],
        compiler_params=_cparams("parallel"),
    )(h, gains, w["win"], w["qg"], w["kvg"], tab, tabq, w["wqn"], w["wqpa"], w["wqpb"], w["wuk"])


def _flash_kernel(q_ref, k_ref, v_ref, o_ref, *, vh):
    qi = pl.program_id(2)
    tq = q_ref.shape[1]
    nq = k_ref.shape[1] // tq

    def attend(n):
        past = n * tq
        row = lax.broadcasted_iota(jnp.int32, (tq, tq), 0)
        col = lax.broadcasted_iota(jnp.int32, (tq, tq), 1)
        per = LANES // vh
        res = []
        for hh in range(q_ref.shape[0]):
            q = q_ref[hh]
            grp = slice(hh // per * LANES, (hh // per + 1) * LANES)
            s = [_dot_nt(q, k_ref[hh, c * tq:(c + 1) * tq, :]) for c in range(n + 1)]
            s[n] = jnp.where(col <= row, s[n], NEG_INF)
            m = functools.reduce(jnp.maximum, [jnp.max(sc, axis=-1, keepdims=True) for sc in s])
            l, o = None, None
            for c in range(n + 1):
                p = jnp.exp(s[c] - m)
                lc = jnp.sum(p, axis=-1, keepdims=True)
                oc = _dot(p.astype(BF16), v_ref[c * tq:(c + 1) * tq, grp])
                l, o = (lc, oc) if l is None else (l + lc, o + oc)
            res.append(o / l)
        lane = lax.broadcasted_iota(jnp.int32, (tq, LANES), 1)
        for g in range(q_ref.shape[0] // per):
            o_ref[:, g * LANES:(g + 1) * LANES] = jnp.where(lane < vh, res[per * g], res[per * g + 1]).astype(BF16)

    for n in range(nq):
        pl.when(qi == n)(functools.partial(attend, n))


def _flash_prompt(q, k, v, batch, seq, tq):
    heads, t, _ = q.shape
    vh = v.shape[1] // heads
    assert LANES // vh == 2 and heads % FLASH_HEADS == 0
    hs = FLASH_HEADS
    vw = hs * vh
    nq = seq // tq
    return pl.pallas_call(
        functools.partial(_flash_kernel, vh=vh),
        grid=(batch, heads // hs, nq),
        in_specs=[
            pl.BlockSpec((hs, tq, LANES), lambda b, p, i: (p, b * nq + i, 0)),
            pl.BlockSpec((hs, seq, LANES), lambda b, p, i: (p, b, 0)),
            pl.BlockSpec((seq, vw), lambda b, p, i: (b, p)),
        ],
        out_specs=pl.BlockSpec((tq, vw), lambda b, p, i: (b * nq + i, p)),
        out_shape=jax.ShapeDtypeStruct((t, heads * vh), BF16),
        compiler_params=_cparams("parallel", "parallel", "arbitrary"),
    )(q, k, v)


def _decode_kernel(pt_ref, qlat_ref, qpe_ref, cnew_ref, knew_ref, lat_hbm, krt_hbm, o_ref,
                   qpe_s, lat_buf, krt_buf, sem, *, pages):
    b = pl.program_id(0)
    slot = b % 2
    heads, nq, lat_w = qlat_ref.shape
    rows = heads * nq
    psz = lat_hbm.shape[1]
    rope = krt_hbm.shape[1]

    def page_copies(seq, to_slot, n):
        page = pt_ref[seq * pages + n]
        dst = pl.ds(pl.multiple_of(n * psz, psz), psz)
        return (pltpu.make_async_copy(lat_hbm.at[page], lat_buf.at[to_slot, dst, :], sem.at[0, to_slot]),
                pltpu.make_async_copy(krt_hbm.at[page], krt_buf.at[to_slot, :, dst], sem.at[1, to_slot]))

    def for_each_page(seq, to_slot, act):
        def body(n, carry):
            for cp in page_copies(seq, to_slot, n):
                act(cp)
            return carry
        lax.fori_loop(0, pages, body, 0)

    @pl.when(b == 0)
    def _():
        for_each_page(0, 0, lambda cp: cp.start())

    @pl.when(b + 1 < pl.num_programs(0))
    def _():
        for_each_page(b + 1, 1 - slot, lambda cp: cp.start())

    for hh in range(heads):
        qpe_s[hh * nq:(hh + 1) * nq, :] = qpe_ref[:, hh * rope:(hh + 1) * rope]
    q = qlat_ref[...].reshape(rows, lat_w).astype(BF16)
    qp = qpe_s[...].astype(BF16)

    cnew = jnp.concatenate([cnew_ref[...], jnp.zeros((psz - nq, lat_w), F32)], axis=0).astype(BF16)
    knew = jnp.concatenate([knew_ref[...], jnp.zeros((psz - nq, rope), F32)], axis=0).astype(BF16)
    s_own = _dot_nt(q, cnew) + _dot_nt(qp, knew)
    qpos = lax.broadcasted_iota(jnp.int32, s_own.shape, 0) % nq
    kpos = lax.broadcasted_iota(jnp.int32, s_own.shape, 1)
    s_own = jnp.where(kpos <= qpos, s_own, NEG_INF)

    for_each_page(b, slot, lambda cp: cp.wait())

    def attend(chunks, s, lats):
        ck = DECODE_KEY_CHUNK
        for c in chunks:
            lat = lat_buf[slot, c * ck:(c + 1) * ck, :].astype(BF16)
            krt = krt_buf[slot, :, c * ck:(c + 1) * ck].astype(BF16)
            lats.append(lat)
            s.append(_dot_nt(q, lat) + _dot(qp, krt))
        m = functools.reduce(jnp.maximum, [jnp.max(sc, axis=-1, keepdims=True) for sc in s])
        l, o = None, None
        for sc, lat in zip(s, lats):
            p = jnp.exp(sc - m)
            lc = jnp.sum(p, axis=-1, keepdims=True)
            oc = _dot(p.astype(BF16), lat)
            l, o = (lc, oc) if l is None else (l + lc, o + oc)
        return m, l, o

    nchunks = pages * psz // DECODE_KEY_CHUNK
    m1, l1, o1 = attend(range(nchunks // 2), [s_own], [cnew])
    m2, l2, o2 = attend(range(nchunks // 2, nchunks), [], [])
    m = jnp.maximum(m1, m2)
    a1, a2 = jnp.exp(m1 - m), jnp.exp(m2 - m)
    o_ref[...] = (o1 * a1 + o2 * a2) / (l1 * a1 + l2 * a2)


def _decode_sample(page_table, qlat, qpe, cnew, knew, lat_pool, krt_pool):
    heads, t, lat_w = qlat.shape
    nb, pages = page_table.shape
    nq = t // nb
    psz = lat_pool.shape[1]
    rope = krt_pool.shape[1]
    rows = heads * nq
    past = pages * psz
    assert past % (2 * DECODE_KEY_CHUNK) == 0
    grid_spec = pltpu.PrefetchScalarGridSpec(
        num_scalar_prefetch=1,
        grid=(nb,),
        in_specs=[
            pl.BlockSpec((heads, nq, lat_w), lambda b, pt_ref: (0, b, 0)),
            pl.BlockSpec((nq, qpe.shape[1]), lambda b, pt_ref: (b, 0)),
            pl.BlockSpec((nq, lat_w), lambda b, pt_ref: (b, 0)),
            pl.BlockSpec((nq, rope), lambda b, pt_ref: (b, 0)),
            pl.BlockSpec(memory_space=pl.ANY),
            pl.BlockSpec(memory_space=pl.ANY),
        ],
        out_specs=pl.BlockSpec((None, rows, lat_w), lambda b, pt_ref: (b, 0, 0)),
        scratch_shapes=[pltpu.VMEM((rows, rope), F32),
                        pltpu.VMEM((2, past, lat_w), F32), pltpu.VMEM((2, rope, past), F32),
                        pltpu.SemaphoreType.DMA((2, 2))],
    )
    return pl.pallas_call(
        functools.partial(_decode_kernel, pages=pages),
        grid_spec=grid_spec,
        out_shape=jax.ShapeDtypeStruct((nb, rows, lat_w), F32),
        compiler_params=_cparams("arbitrary"),
    )(page_table.reshape(-1), qlat, qpe, cnew, knew, lat_pool, krt_pool)


def _mla_post_prompt_kernel(o_ref, h_ref, wo_ref, g_ref, out_ref):
    out_ref[...] = h_ref[...] + _rms(_dot(o_ref[...], wo_ref[...]), g_ref[...])


def _mla_post_prompt(o, h, gains, gi, w_o):
    t, d = h.shape
    tm = _row_tile(t)
    return pl.pallas_call(
        _mla_post_prompt_kernel,
        grid=(t // tm,),
        in_specs=[
            pl.BlockSpec((tm, o.shape[1]), lambda i: (i, 0)),
            pl.BlockSpec((tm, d), lambda i: (i, 0)),
            _full(w_o), _gain_spec(gi, d, 1),
        ],
        out_specs=pl.BlockSpec((tm, d), lambda i: (i, 0)),
        out_shape=jax.ShapeDtypeStruct((t, d), F32),
        compiler_params=_cparams("parallel"),
    )(o, h, w_o, gains)


def _mla_post_sample_kernel(olat_ref, h_ref, wuv_ref, wo_ref, g_ref, out_ref):
    gb, rows, lat_w = olat_ref.shape
    heads = wuv_ref.shape[0]
    nq = rows // heads
    acc = None
    for hh in range(heads):
        x = olat_ref[:, hh * nq:(hh + 1) * nq, :].reshape(gb * nq, lat_w).astype(BF16)
        oh = _dot(x, wuv_ref[hh]).astype(BF16)
        part = _dot(oh, wo_ref[hh])
        acc = part if acc is None else acc + part
    out_ref[...] = h_ref[...] + _rms(acc, g_ref[...])


def _mla_post_sample(olat, h, gains, gi, w_uv, w_o, gb):
    nb, rows, lat_w = olat.shape
    t, d = h.shape
    nq = t // nb
    return pl.pallas_call(
        _mla_post_sample_kernel,
        grid=(nb // gb,),
        in_specs=[
            pl.BlockSpec((gb, rows, lat_w), lambda i: (i, 0, 0)),
            pl.BlockSpec((gb * nq, d), lambda i: (i, 0)),
            _full(w_uv), _full(w_o), _gain_spec(gi, d, 1),
        ],
        out_specs=pl.BlockSpec((gb * nq, d), lambda i: (i, 0)),
        out_shape=jax.ShapeDtypeStruct((t, d), F32),
        compiler_params=_cparams("parallel"),
    )(olat, h, w_uv, w_o, gains)


def _rot_cols(w):
    half = w.shape[-1] // 2
    return jnp.concatenate([-w[..., half:], w[..., :half]], axis=-1)


def _rope_tables(pos, reps):
    inv_freq = ROPE_THETA ** (-jnp.arange(0, QK_ROPE, 2, dtype=F32) / QK_ROPE)
    ang = pos.astype(F32)[:, None] * inv_freq[None, :]
    cos, sin = jnp.cos(ang), jnp.sin(ang)
    p = pos.shape[0]
    pad = LANES - QK_NOPE - QK_ROPE
    cos_g = jnp.concatenate([jnp.ones((p, QK_NOPE), F32), cos, cos, jnp.ones((p, pad), F32)], axis=1)
    sin_g = jnp.concatenate([jnp.zeros((p, QK_NOPE), F32), sin, sin, jnp.zeros((p, pad), F32)], axis=1)
    tab = jnp.stack([cos_g, sin_g])
    cos_q = jnp.tile(jnp.concatenate([cos, cos], axis=1), (1, reps))
    sin_q = jnp.tile(jnp.concatenate([sin, sin], axis=1), (1, reps))
    return tab, jnp.stack([cos_q, sin_q])


def _mla_weights(w_in, q_g, kv_g, w_uq, w_ukv, w_o):
    d = w_in.shape[0]
    heads = w_uq.shape[1]
    pad = LANES - QK_NOPE - QK_ROPE
    wq, wkv, wpe = w_in[:, :Q_LORA], w_in[:, Q_LORA:Q_LORA + KV_LORA], w_in[:, Q_LORA + KV_LORA:]
    zn, zp = jnp.zeros((d, QK_NOPE), F32), jnp.zeros((d, pad), F32)
    win = jnp.concatenate([wq, wkv, zn, wpe, zp, zn, _rot_cols(wpe), zp], axis=1).astype(BF16)
    w_nope, w_pe = w_uq[..., :QK_NOPE], w_uq[..., QK_NOPE:]
    w_pe_rot = _rot_cols(w_pe)
    w_uk, w_uv = w_ukv[..., :QK_NOPE], w_ukv[..., QK_NOPE:]
    vh = w_uv.shape[-1]
    zq_n = jnp.zeros((Q_LORA, heads, QK_NOPE), F32)
    zq_p = jnp.zeros((Q_LORA, heads, pad), F32)
    zk = jnp.zeros((KV_LORA, heads, LANES - QK_NOPE), F32)
    return dict(
        win=win, qg=q_g[None, :], kvg=kv_g[None, :],
        wqa=jnp.concatenate([w_nope, w_pe, zq_p], axis=-1).reshape(Q_LORA, heads * LANES).astype(BF16),
        wqb=jnp.concatenate([zq_n, w_pe_rot, zq_p], axis=-1).reshape(Q_LORA, heads * LANES).astype(BF16),
        wk=jnp.concatenate([w_uk, zk], axis=-1).reshape(KV_LORA, heads * LANES).astype(BF16),
        wv=w_uv.reshape(KV_LORA, heads * vh).astype(BF16),
        wo=w_o.astype(BF16),
        wqn=w_nope.reshape(Q_LORA, heads * QK_NOPE).astype(BF16),
        wqpa=w_pe.reshape(Q_LORA, heads * QK_ROPE).astype(BF16),
        wqpb=w_pe_rot.reshape(Q_LORA, heads * QK_ROPE).astype(BF16),
        wuk=jnp.transpose(w_uk, (1, 2, 0)).astype(BF16),
        wuv=jnp.transpose(w_uv, (1, 0, 2)).astype(BF16),
        wo_h=w_o.reshape(heads, vh, w_o.shape[1]).astype(BF16),
    )


def _trunk(x, is_prompt, W, conv_prevs, mla_pasts, page_table, mem_k, mem_v, mem_rows_per_layer):
    b, s, d = x.shape
    t = b * s
    h = x.reshape(t, d)
    depth = W["ffn_w_out"].shape[0] // 2
    gains = W["gains"]
    heads = W["mla"][0]["wuk"].shape[0]
    tm = _row_tile(t)
    if is_prompt:
        seq_tile = _row_tile(s)
        tab, _ = _rope_tables(jnp.arange(s, dtype=jnp.int32), heads)
        xa_group, xa_ts = 1, seq_tile
    else:
        past_len = page_table.shape[1] * mla_pasts[0][0].shape[1]
        pos = past_len + jnp.arange(s, dtype=jnp.int32)
        tab, tabq = _rope_tables(jnp.tile(pos, tm // s), heads)
        xa_group, xa_ts = min(XATTN_GROUP, b), s
    conv_new, lat_new, rope_new = [], [], []
    for i in range(depth):
        g0 = i * 8
        h = _ffn(h, gains, g0 + 0, g0 + 1, W["ffn_w_in"], W["ffn_w_out"], 2 * i)
        j = i // 2
        if i % 2 == 0:
            u = _conv_a(h, gains, g0 + 2, W["conv_w_pw1"], W["conv_b_pw1"], j)
            u3, h3 = u.reshape(b, s, d), h.reshape(b, s, d)
            cw = W["conv_w_dw"].shape[1]
            if is_prompt:
                hn = _conv_b(h3, u3, None, gains, g0 + 3, W["conv_w_dw"], W["conv_b_dw"], W["conv_ln_g"],
                             W["conv_ln_b"], W["conv_w_pw2"], W["conv_b_pw2"], j, 1, seq_tile)
                conv_new.append(u3[:, s - (cw - 1):])
            else:
                prev = conv_prevs[j]
                hn = _conv_b(h3, u3, prev, gains, g0 + 3, W["conv_w_dw"], W["conv_b_dw"], W["conv_ln_g"],
                             W["conv_ln_b"], W["conv_w_pw2"], W["conv_b_pw2"], j, min(CONV_GROUP, b), s)
                conv_new.append(jnp.concatenate([prev, u3], axis=1)[:, -(cw - 1):])
            h = hn.reshape(t, d)
        else:
            w = W["mla"][j]
            if is_prompt:
                lat, kpe, q, k, v = _mla_pre_prompt(h, gains, g0 + 2, w, tab, seq_tile)
                o = _flash_prompt(q, k, v, b, s, seq_tile)
                h = _mla_post_prompt(o, h, gains, g0 + 3, w["wo"])
            else:
                lat, kpe, qlat, qpe = _mla_pre_sample(h, gains, g0 + 2, w, tab, tabq, tm)
                knew = kpe[:, QK_NOPE:QK_NOPE + QK_ROPE]
                lat_pool, kr_pool = mla_pasts[j]
                olat = _decode_sample(page_table, qlat, qpe, lat, knew, lat_pool, jnp.swapaxes(kr_pool, 1, 2))
                h = _mla_post_sample(olat, h, gains, g0 + 3, w["wuv"], w["wo_h"], min(b, 64))
            lat_new.append(lat.reshape(b, s, KV_LORA))
            rope_new.append(kpe[:, QK_NOPE:QK_NOPE + QK_ROPE].reshape(b, s, QK_ROPE))
        h = _xattn(h.reshape(b, s, d), mem_k, mem_v, i * mem_rows_per_layer, gains, g0 + 4, g0 + 5,
                   W["xa_w_q"], W["xa_w_o"], i, xa_group, xa_ts).reshape(t, d)
        h = _ffn(h, gains, g0 + 6, g0 + 7, W["ffn_w_in"], W["ffn_w_out"], 2 * i + 1)
    return h.reshape(b, s, d), conv_new, lat_new, rope_new


def kernel(x_prompt, x_sample, state_conv_l0, state_conv_l2, cache_mla_latent_l1, cache_mla_krope_l1, cache_mla_latent_l3, cache_mla_krope_l3, cache_mem_k, cache_mem_v, page_table, mem_prompt, norm_gain, ffn_w_in, ffn_w_out, conv_w_pw1, conv_b_pw1, conv_w_dw, conv_b_dw, conv_ln_g, conv_ln_b, conv_w_pw2, conv_b_pw2, mla_w_in, mla_q_norm, mla_kv_norm, mla_w_uq, mla_w_ukv, mla_w_o, xa_mem_norm, xa_w_q, xa_w_kv, xa_w_o):
    depth, _, d = norm_gain.shape
    dff = ffn_w_out.shape[2]
    W = dict(
        gains=norm_gain.reshape(depth * 8, 1, d),
        ffn_w_in=ffn_w_in.reshape(depth * 2, d, 2 * dff).astype(BF16),
        ffn_w_out=ffn_w_out.reshape(depth * 2, dff, d).astype(BF16),
        conv_w_pw1=conv_w_pw1.astype(BF16), conv_b_pw1=conv_b_pw1[:, None, :],
        conv_w_dw=conv_w_dw, conv_b_dw=conv_b_dw[:, None, :],
        conv_ln_g=conv_ln_g[:, None, :], conv_ln_b=conv_ln_b[:, None, :],
        conv_w_pw2=conv_w_pw2.astype(BF16), conv_b_pw2=conv_b_pw2[:, None, :],
        xa_w_q=xa_w_q.astype(BF16), xa_w_o=xa_w_o.astype(BF16),
        mla=[_mla_weights(mla_w_in[j], mla_q_norm[j], mla_kv_norm[j], mla_w_uq[j], mla_w_ukv[j], mla_w_o[j])
             for j in range(mla_w_in.shape[0])],
    )

    bp, n_mem, _ = mem_prompt.shape
    mk, mv = _memkv(mem_prompt.reshape(bp * n_mem, d), xa_mem_norm[:, None, :], xa_w_kv.astype(BF16))
    xw = mk.shape[-1]
    y_prompt, p_conv, p_lat, p_rope = _trunk(
        x_prompt, True, W, None, None, None,
        mk.reshape(depth * bp, n_mem, xw), mv.reshape(depth * bp, n_mem, xw), bp)
    xa_heads = xw // XA_HEAD_DIM
    mem_k_prompt = mk.reshape(depth, bp, n_mem, xa_heads, XA_HEAD_DIM)
    mem_v_prompt = mv.reshape(depth, bp, n_mem, xa_heads, XA_HEAD_DIM)

    bs = x_sample.shape[0]
    y_sample, s_conv, s_lat, s_rope = _trunk(
        x_sample, False, W, [state_conv_l0, state_conv_l2],
        [(cache_mla_latent_l1, cache_mla_krope_l1), (cache_mla_latent_l3, cache_mla_krope_l3)], page_table,
        cache_mem_k.reshape(depth * bs, n_mem * xa_heads, XA_HEAD_DIM),
        cache_mem_v.reshape(depth * bs, n_mem * xa_heads, XA_HEAD_DIM), bs)

    return (y_prompt, y_sample,
            p_conv[0], p_conv[1], p_lat[0], p_rope[0], p_lat[1], p_rope[1],
            mem_k_prompt, mem_v_prompt,
            s_conv[0], s_conv[1], s_lat[0], s_rope[0], s_lat[1], s_rope[1])
```

```python
import functools

import jax
import jax.numpy as jnp
from jax import lax
from jax.experimental import pallas as pl
from jax.experimental.pallas import tpu as pltpu

F32 = jnp.float32
BF16 = jnp.bfloat16

RMS_EPS = 1e-6
LN_EPS = 1e-5
ROPE_THETA = 10000.0
FFN_RESID = 0.5
NEG_INF = -1e30

Q_LORA = 384
KV_LORA = 256
QK_NOPE = 64
QK_ROPE = 32
XA_HEAD_DIM = 128

LANES = 128
SUBLANES = 8
VMEM_LIMIT_BYTES = 52 * 1024 * 1024

ROW_TILE = 512
CONV_HALO = 32
FLASH_HEADS = 8
DECODE_KEY_CHUNK = 1024
DECODE_GROUPS = 2
XATTN_GROUP = 8
CONV_GROUP = 8


def _cparams(*sem):
    return pltpu.CompilerParams(dimension_semantics=sem, vmem_limit_bytes=VMEM_LIMIT_BYTES)


def _row_tile(t):
    return ROW_TILE if t % ROW_TILE == 0 else t


def _rms(x, g):
    return x * lax.rsqrt(jnp.mean(x * x, axis=-1, keepdims=True) + RMS_EPS) * g


def _dot(a, b):
    return jnp.dot(a, b, preferred_element_type=F32)


def _dot_nt(a, b):
    return lax.dot_general(a, b, (((1,), (1,)), ((), ())), preferred_element_type=F32)


def _dot_tn(a, b):
    return lax.dot_general(a, b, (((0,), (0,)), ((), ())), preferred_element_type=F32)


def _gain_spec(idx, d, ngrid):
    if ngrid == 1:
        return pl.BlockSpec((None, 1, d), lambda i: (idx, 0, 0))
    return pl.BlockSpec((None, 1, d), lambda i, j: (idx, 0, 0))


def _ffn_kernel(h_ref, gpre_ref, gpost_ref, wg_ref, wu_ref, wo_ref, out_ref, xn_ref, acc_ref):
    j = pl.program_id(1)

    @pl.when(j == 0)
    def _():
        xn_ref[...] = _rms(h_ref[...], gpre_ref[...]).astype(BF16)
        acc_ref[...] = jnp.zeros_like(acc_ref)

    tf = wo_ref.shape[0]
    gu = _dot(xn_ref[...], jnp.concatenate([wg_ref[...], wu_ref[...]], axis=1))
    gate, up = gu[:, :tf], gu[:, tf:]
    a = (gate * jax.nn.sigmoid(gate) * up).astype(BF16)
    acc_ref[...] += _dot(a, wo_ref[...])

    @pl.when(j == pl.num_programs(1) - 1)
    def _():
        out_ref[...] = h_ref[...] + FFN_RESID * _rms(acc_ref[...], gpost_ref[...])


def _ffn_chunk(f):
    return f // 2 if (f // 2) % LANES == 0 else f


def _ffn(h, gains, gi_pre, gi_post, w_in, w_out, li):
    t, d = h.shape
    f = w_out.shape[1]
    tm = _row_tile(t)
    tf = _ffn_chunk(f)
    nf = f // tf
    return pl.pallas_call(
        _ffn_kernel,
        grid=(t // tm, nf),
        in_specs=[
            pl.BlockSpec((tm, d), lambda i, j: (i, 0)),
            _gain_spec(gi_pre, d, 2),
            _gain_spec(gi_post, d, 2),
            pl.BlockSpec((None, d, tf), lambda i, j: (li, 0, j)),
            pl.BlockSpec((None, d, tf), lambda i, j: (li, 0, j + nf)),
            pl.BlockSpec((None, tf, d), lambda i, j: (li, j, 0)),
        ],
        out_specs=pl.BlockSpec((tm, d), lambda i, j: (i, 0)),
        out_shape=jax.ShapeDtypeStruct((t, d), F32),
        scratch_shapes=[pltpu.VMEM((tm, d), BF16), pltpu.VMEM((tm, d), F32)],
        compiler_params=_cparams("parallel", "arbitrary"),
    )(h, gains, gains, w_in, w_in, w_out)


def _memkv_kernel(mem_ref, g_ref, w_ref, k_ref, v_ref):
    xn = _rms(mem_ref[...], g_ref[...]).astype(BF16)
    kv = _dot(xn, w_ref[...])
    xw = k_ref.shape[-1]
    k_ref[...] = kv[:, :xw]
    v_ref[...] = kv[:, xw:]


def _memkv(mem, gains, w_kv):
    m, d = mem.shape
    depth, _, xw2 = w_kv.shape
    xw = xw2 // 2
    tm = _row_tile(m)
    out = jax.ShapeDtypeStruct((depth, m, xw), F32)
    return pl.pallas_call(
        _memkv_kernel,
        grid=(depth, m // tm),
        in_specs=[
            pl.BlockSpec((tm, d), lambda l, i: (i, 0)),
            pl.BlockSpec((None, 1, d), lambda l, i: (l, 0, 0)),
            pl.BlockSpec((None, d, xw2), lambda l, i: (l, 0, 0)),
        ],
        out_specs=[pl.BlockSpec((None, tm, xw), lambda l, i: (l, i, 0))] * 2,
        out_shape=[out, out],
        compiler_params=_cparams("parallel", "parallel"),
    )(mem, gains, w_kv)


def _softmax_pv(s, v):
    e = jnp.exp(s - jnp.max(s, axis=-1, keepdims=True))
    l = jnp.sum(e, axis=-1, keepdims=True)
    return jnp.einsum("gqk,gkd->gqd", e.astype(BF16), v, preferred_element_type=F32) / l


def _xattn_kernel(h_ref, k_ref, v_ref, gpre_ref, gpost_ref, wq_ref, wo_ref, out_ref, *, heads):
    g, ts, d = h_ref.shape
    xw = wq_ref.shape[-1]
    hd = xw // heads
    h = h_ref[...].reshape(g * ts, d)
    xn = _rms(h, gpre_ref[...]).astype(BF16)
    q = (_dot(xn, wq_ref[...]) * (hd ** -0.5)).reshape(g, ts, xw)
    k = k_ref[...].astype(BF16)
    v = v_ref[...].astype(BF16)
    if k_ref.shape[-1] == xw:
        q = q.astype(BF16)
        outs = []
        for hh in range(heads):
            sl = slice(hh * hd, (hh + 1) * hd)
            s = jnp.einsum("gqd,gkd->gqk", q[:, :, sl], k[:, :, sl], preferred_element_type=F32)
            outs.append(_softmax_pv(s, v[:, :, sl]))
    else:
        qs = jnp.concatenate([q[:, :, hh * hd:(hh + 1) * hd] for hh in range(heads)], axis=1).astype(BF16)
        s = jnp.einsum("gqd,gkd->gqk", qs, k, preferred_element_type=F32)
        q_head = lax.broadcasted_iota(jnp.int32, s.shape, 1) // ts
        k_head = lax.broadcasted_iota(jnp.int32, s.shape, 2) % heads
        o = _softmax_pv(jnp.where(q_head == k_head, s, NEG_INF), v)
        outs = [o[:, hh * ts:(hh + 1) * ts, :] for hh in range(heads)]
    o = jnp.concatenate(outs, axis=-1).reshape(g * ts, xw).astype(BF16)
    c = _dot(o, wo_ref[...])
    out_ref[...] = (h + _rms(c, gpost_ref[...])).reshape(g, ts, d)


def _xattn(h, k, v, kv_base, gains, gi_pre, gi_post, w_q, w_o, li, group, ts):
    b, s, d = h.shape
    _, m, kw = k.shape
    xw = w_q.shape[-1]
    heads = xw // XA_HEAD_DIM
    kvb = kv_base // group
    return pl.pallas_call(
        functools.partial(_xattn_kernel, heads=heads),
        grid=(b // group, s // ts),
        in_specs=[
            pl.BlockSpec((group, ts, d), lambda i, j: (i, j, 0)),
            pl.BlockSpec((group, m, kw), lambda i, j: (kvb + i, 0, 0)),
            pl.BlockSpec((group, m, kw), lambda i, j: (kvb + i, 0, 0)),
            _gain_spec(gi_pre, d, 2),
            _gain_spec(gi_post, d, 2),
            pl.BlockSpec((None, d, xw), lambda i, j: (li, 0, 0)),
            pl.BlockSpec((None, xw, d), lambda i, j: (li, 0, 0)),
        ],
        out_specs=pl.BlockSpec((group, ts, d), lambda i, j: (i, j, 0)),
        out_shape=jax.ShapeDtypeStruct((b, s, d), F32),
        compiler_params=_cparams("parallel", "parallel"),
    )(h, k, v, gains, gains, w_q, w_o)


def _conv_a_kernel(h_ref, g_ref, w_ref, b_ref, u_ref):
    d = u_ref.shape[-1]
    xn = _rms(h_ref[...], g_ref[...]).astype(BF16)
    y = _dot(xn, w_ref[...]) + b_ref[...]
    u_ref[...] = y[:, :d] * jax.nn.sigmoid(y[:, d:])


def _conv_a(h, gains, gi, w_pw1, b_pw1, lj):
    t, d = h.shape
    tm = _row_tile(t)
    return pl.pallas_call(
        _conv_a_kernel,
        grid=(t // tm,),
        in_specs=[
            pl.BlockSpec((tm, d), lambda i: (i, 0)),
            _gain_spec(gi, d, 1),
            pl.BlockSpec((None, d, 2 * d), lambda i: (lj, 0, 0)),
            pl.BlockSpec((None, 1, 2 * d), lambda i: (lj, 0, 0)),
        ],
        out_specs=pl.BlockSpec((tm, d), lambda i: (i, 0)),
        out_shape=jax.ShapeDtypeStruct((t, d), F32),
        compiler_params=_cparams("parallel"),
    )(h, gains, w_pw1, b_pw1)


def _conv_b_kernel(h_ref, u_ref, prev_ref, wdw_ref, bdw_ref, lng_ref, lnb_ref, w2_ref, b2_ref,
                   gpost_ref, out_ref, e_ref, y_ref, wb_ref, *, width, from_prev_tile):
    g, ts, d = u_ref.shape
    lead = CONV_HALO - (width - 1)
    ext_rows = CONV_HALO + ts

    if from_prev_tile:
        first = pl.program_id(1) == 0

        @pl.when(first)
        def _():
            e_ref[0, :, :CONV_HALO, :] = jnp.zeros((g, CONV_HALO, d), F32)

        @pl.when(jnp.logical_not(first))
        def _():
            e_ref[0, :, :CONV_HALO, :] = prev_ref[...]
    else:
        e_ref[0, :, :lead, :] = jnp.zeros((g, lead, d), F32)
        e_ref[0, :, lead:CONV_HALO, :] = prev_ref[...]
    e_ref[0, :, CONV_HALO:, :] = u_ref[...]

    span = ext_rows - SUBLANES
    for r in range(1, SUBLANES):
        e_ref[r, :, :span, :] = e_ref[0, :, r:r + span, :]

    for k in range(width):
        wb_ref[k] = jnp.broadcast_to(wdw_ref[k:k + 1, :], (SUBLANES, d))

    sub = 2 if ts % (2 * SUBLANES) == 0 else 1
    chunk_rows = sub * SUBLANES
    chunks_per_seq = ts // chunk_rows

    def chunk(c, carry):
        gi = c // chunks_per_seq
        t0 = pl.multiple_of((c % chunks_per_seq) * chunk_rows, chunk_rows)
        accs = [jnp.broadcast_to(bdw_ref[...], (SUBLANES, d))] * sub
        for k in range(width):
            off = lead + k
            r = off % SUBLANES
            w = wb_ref[k]
            for a in range(sub):
                start = pl.multiple_of(t0 + (off - r) + a * SUBLANES, SUBLANES)
                accs[a] = accs[a] + e_ref[r, gi, pl.ds(start, SUBLANES), :] * w
        for a in range(sub):
            y_ref[gi, pl.ds(pl.multiple_of(t0 + a * SUBLANES, SUBLANES), SUBLANES), :] = accs[a]
        return carry

    lax.fori_loop(0, g * chunks_per_seq, chunk, 0)

    y = y_ref[...].reshape(g * ts, d)
    mu = jnp.mean(y, axis=-1, keepdims=True)
    var = jnp.mean(jnp.square(y - mu), axis=-1, keepdims=True)
    z = (y - mu) * lax.rsqrt(var + LN_EPS) * lng_ref[...] + lnb_ref[...]
    z = (z * jax.nn.sigmoid(z)).astype(BF16)
    m = _dot(z, w2_ref[...]) + b2_ref[...]
    h = h_ref[...].reshape(g * ts, d)
    out_ref[...] = (h + _rms(m, gpost_ref[...])).reshape(g, ts, d)


def _conv_b(h, u, prev, gains, gi_post, w_dw, b_dw, ln_g, ln_b, w_pw2, b_pw2, lj, group, ts):
    b, s, d = u.shape
    width = w_dw.shape[1]
    from_prev_tile = prev is None
    if from_prev_tile:
        assert ts % CONV_HALO == 0 and group == 1
        per = ts // CONV_HALO
        prev_arg = u
        prev_spec = pl.BlockSpec((group, CONV_HALO, d), lambda i, j: (i, jnp.maximum(j * per - 1, 0), 0))
    else:
        assert ts == s
        prev_arg = prev
        prev_spec = pl.BlockSpec((group, width - 1, d), lambda i, j: (i, 0, 0))
    vec = lambda: pl.BlockSpec((None, 1, d), lambda i, j: (lj, 0, 0))
    return pl.pallas_call(
        functools.partial(_conv_b_kernel, width=width, from_prev_tile=from_prev_tile),
        grid=(b // group, s // ts),
        in_specs=[
            pl.BlockSpec((group, ts, d), lambda i, j: (i, j, 0)),
            pl.BlockSpec((group, ts, d), lambda i, j: (i, j, 0)),
            prev_spec,
            pl.BlockSpec((None, width, d), lambda i, j: (lj, 0, 0)),
            vec(), vec(), vec(),
            pl.BlockSpec((None, d, d), lambda i, j: (lj, 0, 0)),
            vec(),
            _gain_spec(gi_post, d, 2),
        ],
        out_specs=pl.BlockSpec((group, ts, d), lambda i, j: (i, j, 0)),
        out_shape=jax.ShapeDtypeStruct((b, s, d), F32),
        scratch_shapes=[pltpu.VMEM((SUBLANES, group, CONV_HALO + ts, d), F32),
                        pltpu.VMEM((group, ts, d), F32),
                        pltpu.VMEM((width, SUBLANES, d), F32)],
        compiler_params=_cparams("parallel", "arbitrary"),
    )(h, u, prev_arg, w_dw, b_dw, ln_g, ln_b, w_pw2, b_pw2, gains)


def _mla_down(h_ref, g_ref, win_ref, qg_ref, kvg_ref, tab_ref, lat_ref, kpe_ref):
    xn = _rms(h_ref[...], g_ref[...]).astype(BF16)
    down = _dot(xn, win_ref[...])
    c_q = _rms(down[:, :Q_LORA], qg_ref[...])
    c_kv = _rms(down[:, Q_LORA:Q_LORA + KV_LORA], kvg_ref[...])
    base = Q_LORA + KV_LORA
    kpe = (down[:, base:base + LANES] * tab_ref[0] + down[:, base + LANES:base + 2 * LANES] * tab_ref[1])
    lat_ref[...] = c_kv
    kpe_ref[...] = kpe
    return c_q.astype(BF16), c_kv.astype(BF16), kpe


def _mla_pre_prompt_kernel(h_ref, g_ref, win_ref, qg_ref, kvg_ref, tab_ref, wqa_ref, wqb_ref, wk_ref, wv_ref,
                           lat_ref, kpe_ref, q_ref, k_ref, v_ref, *, scale):
    c_q, c_kv, kpe = _mla_down(h_ref, g_ref, win_ref, qg_ref, kvg_ref, tab_ref, lat_ref, kpe_ref)
    qa = _dot(c_q, wqa_ref[...])
    qb = _dot(c_q, wqb_ref[...])
    ka = _dot(c_kv, wk_ref[...])
    v_ref[...] = _dot(c_kv, wv_ref[...]).astype(BF16)
    cos = tab_ref[0] * scale
    sin = tab_ref[1] * scale
    for hh in range(q_ref.shape[0]):
        sl = slice(hh * LANES, (hh + 1) * LANES)
        q_ref[hh] = (qa[:, sl] * cos + qb[:, sl] * sin).astype(BF16)
        k_ref[hh] = (ka[:, sl] + kpe).astype(BF16)


def _mla_pre_sample_kernel(h_ref, g_ref, win_ref, qg_ref, kvg_ref, tab_ref, tabq_ref, wqn_ref, wqa_ref, wqb_ref,
                           wuk_ref, lat_ref, kpe_ref, qlat_ref, qpe_ref, *, scale):
    c_q, _, _ = _mla_down(h_ref, g_ref, win_ref, qg_ref, kvg_ref, tab_ref, lat_ref, kpe_ref)
    qn = _dot(c_q, wqn_ref[...]).astype(BF16)
    qpe = _dot(c_q, wqa_ref[...]) * tabq_ref[0] + _dot(c_q, wqb_ref[...]) * tabq_ref[1]
    qpe_ref[...] = qpe * scale
    nope = wuk_ref.shape[1]
    for hh in range(qlat_ref.shape[0]):
        qlat_ref[hh] = _dot(qn[:, hh * nope:(hh + 1) * nope], wuk_ref[hh]) * scale


def _full(a):
    nd = a.ndim
    return pl.BlockSpec(a.shape, lambda i: (0,) * nd)


def _mla_pre_prompt(h, gains, gi, w, tab, tm):
    t, d = h.shape
    heads = w["wqa"].shape[1] // LANES
    nper = tab.shape[1] // tm
    scale = float((QK_NOPE + QK_ROPE) ** -0.5)
    row = lambda width: pl.BlockSpec((tm, width), lambda i: (i, 0))
    return pl.pallas_call(
        functools.partial(_mla_pre_prompt_kernel, scale=scale),
        grid=(t // tm,),
        in_specs=[
            row(d), _gain_spec(gi, d, 1), _full(w["win"]), _full(w["qg"]), _full(w["kvg"]),
            pl.BlockSpec((2, tm, LANES), lambda i: (0, i % nper, 0)),
            _full(w["wqa"]), _full(w["wqb"]), _full(w["wk"]), _full(w["wv"]),
        ],
        out_specs=[
            row(KV_LORA), row(LANES),
            pl.BlockSpec((heads, tm, LANES), lambda i: (0, i, 0)),
            pl.BlockSpec((heads, tm, LANES), lambda i: (0, i, 0)),
            row(w["wv"].shape[1]),
        ],
        out_shape=[
            jax.ShapeDtypeStruct((t, KV_LORA), F32), jax.ShapeDtypeStruct((t, LANES), F32),
            jax.ShapeDtypeStruct((heads, t, LANES), BF16), jax.ShapeDtypeStruct((heads, t, LANES), BF16),
            jax.ShapeDtypeStruct((t, w["wv"].shape[1]), BF16),
        ],
        compiler_params=_cparams("parallel"),
    )(h, gains, w["win"], w["qg"], w["kvg"], tab, w["wqa"], w["wqb"], w["wk"], w["wv"])


def _mla_pre_sample(h, gains, gi, w, tab, tabq, tm):
    t, d = h.shape
    heads = w["wuk"].shape[0]
    pe_w = w["wqpa"].shape[1]
    scale = float((QK_NOPE + QK_ROPE) ** -0.5)
    row = lambda width: pl.BlockSpec((tm, width), lambda i: (i, 0))
    return pl.pallas_call(
        functools.partial(_mla_pre_sample_kernel, scale=scale),
        grid=(t // tm,),
        in_specs=[
            row(d), _gain_spec(gi, d, 1), _full(w["win"]), _full(w["qg"]), _full(w["kvg"]),
            _full(tab), _full(tabq),
            _full(w["wqn"]), _full(w["wqpa"]), _full(w["wqpb"]), _full(w["wuk"]),
        ],
        out_specs=[
            row(KV_LORA), row(LANES),
            pl.BlockSpec((heads, tm, KV_LORA), lambda i: (0, i, 0)),
            row(pe_w),
        ],
        out_shape=[
            jax.ShapeDtypeStruct((t, KV_LORA), F32), jax.ShapeDtypeStruct((t, LANES), F32),
            jax.ShapeDtypeStruct((heads, t, KV_LORA), F32), jax.ShapeDtypeStruct((t, pe_w), F32),
        ],
        compiler_params=_cparams("parallel"),
    )(h, gains, w["win"], w["qg"], w["kvg"], tab, tabq, w["wqn"], w["wqpa"], w["wqpb"], w["wuk"])


def _flash_kernel(q_ref, k_ref, v_ref, o_ref, *, vh):
    qi = pl.program_id(2)
    tq = q_ref.shape[1]
    nq = k_ref.shape[1] // tq

    def attend(n):
        past = n * tq
        row = lax.broadcasted_iota(jnp.int32, (tq, tq), 0)
        col = lax.broadcasted_iota(jnp.int32, (tq, tq), 1)
        per = LANES // vh
        res = []
        for hh in range(q_ref.shape[0]):
            q = q_ref[hh]
            grp = slice(hh // per * LANES, (hh // per + 1) * LANES)
            s = [_dot_nt(q, k_ref[hh, c * tq:(c + 1) * tq, :]) for c in range(n + 1)]
            s[n] = jnp.where(col <= row, s[n], NEG_INF)
            m = functools.reduce(jnp.maximum, [jnp.max(sc, axis=-1, keepdims=True) for sc in s])
            l, o = None, None
            for c in range(n + 1):
                p = jnp.exp(s[c] - m)
                lc = jnp.sum(p, axis=-1, keepdims=True)
                oc = _dot(p.astype(BF16), v_ref[c * tq:(c + 1) * tq, grp])
                l, o = (lc, oc) if l is None else (l + lc, o + oc)
            res.append(o / l)
        lane = lax.broadcasted_iota(jnp.int32, (tq, LANES), 1)
        for g in range(q_ref.shape[0] // per):
            o_ref[:, g * LANES:(g + 1) * LANES] = jnp.where(lane < vh, res[per * g], res[per * g + 1]).astype(BF16)

    for n in range(nq):
        pl.when(qi == n)(functools.partial(attend, n))


def _flash_prompt(q, k, v, batch, seq, tq):
    heads, t, _ = q.shape
    vh = v.shape[1] // heads
    assert LANES // vh == 2 and heads % FLASH_HEADS == 0
    hs = FLASH_HEADS
    vw = hs * vh
    nq = seq // tq
    return pl.pallas_call(
        functools.partial(_flash_kernel, vh=vh),
        grid=(batch, heads // hs, nq),
        in_specs=[
            pl.BlockSpec((hs, tq, LANES), lambda b, p, i: (p, b * nq + i, 0)),
            pl.BlockSpec((hs, seq, LANES), lambda b, p, i: (p, b, 0)),
            pl.BlockSpec((seq, vw), lambda b, p, i: (b, p)),
        ],
        out_specs=pl.BlockSpec((tq, vw), lambda b, p, i: (b * nq + i, p)),
        out_shape=jax.ShapeDtypeStruct((t, heads * vh), BF16),
        compiler_params=_cparams("parallel", "parallel", "arbitrary"),
    )(q, k, v)


def _decode_kernel(pt_ref, qlat_ref, qpe_ref, cnew_ref, knew_ref, lat_hbm, krt_hbm, o_ref,
                   qpe_s, lat_buf, krt_buf, latt_s, sem, *, pages):
    b = pl.program_id(0)
    slot = b % 2
    heads, nq, lat_w = qlat_ref.shape
    rows = heads * nq
    psz = lat_hbm.shape[1]
    rope = krt_hbm.shape[1]

    def page_copies(seq, to_slot, n):
        page = pt_ref[seq * pages + n]
        dst = pl.ds(pl.multiple_of(n * psz, psz), psz)
        return (pltpu.make_async_copy(lat_hbm.at[page], lat_buf.at[to_slot, dst, :], sem.at[0, to_slot]),
                pltpu.make_async_copy(krt_hbm.at[page], krt_buf.at[to_slot, :, dst], sem.at[1, to_slot]))

    def for_each_page(seq, to_slot, act):
        def body(n, carry):
            for cp in page_copies(seq, to_slot, n):
                act(cp)
            return carry
        lax.fori_loop(0, pages, body, 0)

    @pl.when(b == 0)
    def _():
        for_each_page(0, 0, lambda cp: cp.start())

    @pl.when(b + 1 < pl.num_programs(0))
    def _():
        for_each_page(b + 1, 1 - slot, lambda cp: cp.start())

    for hh in range(heads):
        qpe_s[hh * nq:(hh + 1) * nq, :] = qpe_ref[:, hh * rope:(hh + 1) * rope]
    q = qlat_ref[...].reshape(rows, lat_w).astype(BF16)
    qp = qpe_s[...].astype(BF16)

    cnew = jnp.concatenate([cnew_ref[...], jnp.zeros((psz - nq, lat_w), F32)], axis=0).astype(BF16)
    knew = jnp.concatenate([knew_ref[...], jnp.zeros((psz - nq, rope), F32)], axis=0).astype(BF16)
    s_own = _dot_nt(q, cnew) + _dot_nt(qp, knew)
    qpos = lax.broadcasted_iota(jnp.int32, s_own.shape, 0) % nq
    kpos = lax.broadcasted_iota(jnp.int32, s_own.shape, 1)
    s_own = jnp.where(kpos <= qpos, s_own, NEG_INF)

    for_each_page(b, slot, lambda cp: cp.wait())

    def attend(chunks, s, lats):
        ck = DECODE_KEY_CHUNK
        for c in chunks:
            lat = lat_buf[slot, c * ck:(c + 1) * ck, :].astype(BF16)
            krt = krt_buf[slot, :, c * ck:(c + 1) * ck].astype(BF16)
            lats.append(lat)
            latt_s[c] = lat.T
            s.append(_dot(q, latt_s[c]) + _dot(qp, krt))
        m = functools.reduce(jnp.maximum, [jnp.max(sc, axis=-1, keepdims=True) for sc in s])
        l, o = None, None
        for sc, lat in zip(s, lats):
            p = jnp.exp(sc - m)
            lc = jnp.sum(p, axis=-1, keepdims=True)
            oc = _dot(p.astype(BF16), lat)
            l, o = (lc, oc) if l is None else (l + lc, o + oc)
        return m, l, o

    per = pages * psz // DECODE_KEY_CHUNK // DECODE_GROUPS
    parts = [attend(range(g * per, (g + 1) * per), [s_own] if g == 0 else [], [cnew] if g == 0 else [])
             for g in range(DECODE_GROUPS)]
    m = functools.reduce(jnp.maximum, [mg for mg, _, _ in parts])
    scales = [jnp.exp(mg - m) for mg, _, _ in parts]
    num = sum(og * a for (_, _, og), a in zip(parts, scales))
    den = sum(lg * a for (_, lg, _), a in zip(parts, scales))
    o_ref[...] = num / den


def _decode_sample(page_table, qlat, qpe, cnew, knew, lat_pool, krt_pool):
    heads, t, lat_w = qlat.shape
    nb, pages = page_table.shape
    nq = t // nb
    psz = lat_pool.shape[1]
    rope = krt_pool.shape[1]
    rows = heads * nq
    past = pages * psz
    assert past % (DECODE_GROUPS * DECODE_KEY_CHUNK) == 0
    grid_spec = pltpu.PrefetchScalarGridSpec(
        num_scalar_prefetch=1,
        grid=(nb,),
        in_specs=[
            pl.BlockSpec((heads, nq, lat_w), lambda b, pt_ref: (0, b, 0)),
            pl.BlockSpec((nq, qpe.shape[1]), lambda b, pt_ref: (b, 0)),
            pl.BlockSpec((nq, lat_w), lambda b, pt_ref: (b, 0)),
            pl.BlockSpec((nq, rope), lambda b, pt_ref: (b, 0)),
            pl.BlockSpec(memory_space=pl.ANY),
            pl.BlockSpec(memory_space=pl.ANY),
        ],
        out_specs=pl.BlockSpec((None, rows, lat_w), lambda b, pt_ref: (b, 0, 0)),
        scratch_shapes=[pltpu.VMEM((rows, rope), F32),
                        pltpu.VMEM((2, past, lat_w), F32), pltpu.VMEM((2, rope, past), F32),
                        pltpu.VMEM((past // DECODE_KEY_CHUNK, lat_w, DECODE_KEY_CHUNK), BF16),
                        pltpu.SemaphoreType.DMA((2, 2))],
    )
    return pl.pallas_call(
        functools.partial(_decode_kernel, pages=pages),
        grid_spec=grid_spec,
        out_shape=jax.ShapeDtypeStruct((nb, rows, lat_w), F32),
        compiler_params=_cparams("arbitrary"),
    )(page_table.reshape(-1), qlat, qpe, cnew, knew, lat_pool, krt_pool)


def _mla_post_prompt_kernel(o_ref, h_ref, wo_ref, g_ref, out_ref):
    out_ref[...] = h_ref[...] + _rms(_dot(o_ref[...], wo_ref[...]), g_ref[...])


def _mla_post_prompt(o, h, gains, gi, w_o):
    t, d = h.shape
    tm = _row_tile(t)
    return pl.pallas_call(
        _mla_post_prompt_kernel,
        grid=(t // tm,),
        in_specs=[
            pl.BlockSpec((tm, o.shape[1]), lambda i: (i, 0)),
            pl.BlockSpec((tm, d), lambda i: (i, 0)),
            _full(w_o), _gain_spec(gi, d, 1),
        ],
        out_specs=pl.BlockSpec((tm, d), lambda i: (i, 0)),
        out_shape=jax.ShapeDtypeStruct((t, d), F32),
        compiler_params=_cparams("parallel"),
    )(o, h, w_o, gains)


def _mla_post_sample_kernel(olat_ref, h_ref, wuv_ref, wo_ref, g_ref, out_ref):
    gb, rows, lat_w = olat_ref.shape
    heads = wuv_ref.shape[0]
    nq = rows // heads
    acc = None
    for hh in range(heads):
        x = olat_ref[:, hh * nq:(hh + 1) * nq, :].reshape(gb * nq, lat_w).astype(BF16)
        oh = _dot(x, wuv_ref[hh]).astype(BF16)
        part = _dot(oh, wo_ref[hh])
        acc = part if acc is None else acc + part
    out_ref[...] = h_ref[...] + _rms(acc, g_ref[...])


def _mla_post_sample(olat, h, gains, gi, w_uv, w_o, gb):
    nb, rows, lat_w = olat.shape
    t, d = h.shape
    nq = t // nb
    return pl.pallas_call(
        _mla_post_sample_kernel,
        grid=(nb // gb,),
        in_specs=[
            pl.BlockSpec((gb, rows, lat_w), lambda i: (i, 0, 0)),
            pl.BlockSpec((gb * nq, d), lambda i: (i, 0)),
            _full(w_uv), _full(w_o), _gain_spec(gi, d, 1),
        ],
        out_specs=pl.BlockSpec((gb * nq, d), lambda i: (i, 0)),
        out_shape=jax.ShapeDtypeStruct((t, d), F32),
        compiler_params=_cparams("parallel"),
    )(olat, h, w_uv, w_o, gains)


def _rot_cols(w):
    half = w.shape[-1] // 2
    return jnp.concatenate([-w[..., half:], w[..., :half]], axis=-1)


def _rope_tables(pos, reps):
    inv_freq = ROPE_THETA ** (-jnp.arange(0, QK_ROPE, 2, dtype=F32) / QK_ROPE)
    ang = pos.astype(F32)[:, None] * inv_freq[None, :]
    cos, sin = jnp.cos(ang), jnp.sin(ang)
    p = pos.shape[0]
    pad = LANES - QK_NOPE - QK_ROPE
    cos_g = jnp.concatenate([jnp.ones((p, QK_NOPE), F32), cos, cos, jnp.ones((p, pad), F32)], axis=1)
    sin_g = jnp.concatenate([jnp.zeros((p, QK_NOPE), F32), sin, sin, jnp.zeros((p, pad), F32)], axis=1)
    tab = jnp.stack([cos_g, sin_g])
    cos_q = jnp.tile(jnp.concatenate([cos, cos], axis=1), (1, reps))
    sin_q = jnp.tile(jnp.concatenate([sin, sin], axis=1), (1, reps))
    return tab, jnp.stack([cos_q, sin_q])


def _mla_weights(w_in, q_g, kv_g, w_uq, w_ukv, w_o):
    d = w_in.shape[0]
    heads = w_uq.shape[1]
    pad = LANES - QK_NOPE - QK_ROPE
    wq, wkv, wpe = w_in[:, :Q_LORA], w_in[:, Q_LORA:Q_LORA + KV_LORA], w_in[:, Q_LORA + KV_LORA:]
    zn, zp = jnp.zeros((d, QK_NOPE), F32), jnp.zeros((d, pad), F32)
    win = jnp.concatenate([wq, wkv, zn, wpe, zp, zn, _rot_cols(wpe), zp], axis=1).astype(BF16)
    w_nope, w_pe = w_uq[..., :QK_NOPE], w_uq[..., QK_NOPE:]
    w_pe_rot = _rot_cols(w_pe)
    w_uk, w_uv = w_ukv[..., :QK_NOPE], w_ukv[..., QK_NOPE:]
    vh = w_uv.shape[-1]
    zq_n = jnp.zeros((Q_LORA, heads, QK_NOPE), F32)
    zq_p = jnp.zeros((Q_LORA, heads, pad), F32)
    zk = jnp.zeros((KV_LORA, heads, LANES - QK_NOPE), F32)
    return dict(
        win=win, qg=q_g[None, :], kvg=kv_g[None, :],
        wqa=jnp.concatenate([w_nope, w_pe, zq_p], axis=-1).reshape(Q_LORA, heads * LANES).astype(BF16),
        wqb=jnp.concatenate([zq_n, w_pe_rot, zq_p], axis=-1).reshape(Q_LORA, heads * LANES).astype(BF16),
        wk=jnp.concatenate([w_uk, zk], axis=-1).reshape(KV_LORA, heads * LANES).astype(BF16),
        wv=w_uv.reshape(KV_LORA, heads * vh).astype(BF16),
        wo=w_o.astype(BF16),
        wqn=w_nope.reshape(Q_LORA, heads * QK_NOPE).astype(BF16),
        wqpa=w_pe.reshape(Q_LORA, heads * QK_ROPE).astype(BF16),
        wqpb=w_pe_rot.reshape(Q_LORA, heads * QK_ROPE).astype(BF16),
        wuk=jnp.transpose(w_uk, (1, 2, 0)).astype(BF16),
        wuv=jnp.transpose(w_uv, (1, 0, 2)).astype(BF16),
        wo_h=w_o.reshape(heads, vh, w_o.shape[1]).astype(BF16),
    )


def _trunk(x, is_prompt, W, conv_prevs, mla_pasts, page_table, mem_k, mem_v, mem_rows_per_layer):
    b, s, d = x.shape
    t = b * s
    h = x.reshape(t, d)
    depth = W["ffn_w_out"].shape[0] // 2
    gains = W["gains"]
    heads = W["mla"][0]["wuk"].shape[0]
    tm = _row_tile(t)
    if is_prompt:
        seq_tile = _row_tile(s)
        tab, _ = _rope_tables(jnp.arange(s, dtype=jnp.int32), heads)
        xa_group, xa_ts = 1, seq_tile
    else:
        past_len = page_table.shape[1] * mla_pasts[0][0].shape[1]
        pos = past_len + jnp.arange(s, dtype=jnp.int32)
        tab, tabq = _rope_tables(jnp.tile(pos, tm // s), heads)
        xa_group, xa_ts = min(XATTN_GROUP, b), s
    conv_new, lat_new, rope_new = [], [], []
    for i in range(depth):
        g0 = i * 8
        h = _ffn(h, gains, g0 + 0, g0 + 1, W["ffn_w_in"], W["ffn_w_out"], 2 * i)
        j = i // 2
        if i % 2 == 0:
            u = _conv_a(h, gains, g0 + 2, W["conv_w_pw1"], W["conv_b_pw1"], j)
            u3, h3 = u.reshape(b, s, d), h.reshape(b, s, d)
            cw = W["conv_w_dw"].shape[1]
            if is_prompt:
                hn = _conv_b(h3, u3, None, gains, g0 + 3, W["conv_w_dw"], W["conv_b_dw"], W["conv_ln_g"],
                             W["conv_ln_b"], W["conv_w_pw2"], W["conv_b_pw2"], j, 1, seq_tile)
                conv_new.append(u3[:, s - (cw - 1):])
            else:
                prev = conv_prevs[j]
                hn = _conv_b(h3, u3, prev, gains, g0 + 3, W["conv_w_dw"], W["conv_b_dw"], W["conv_ln_g"],
                             W["conv_ln_b"], W["conv_w_pw2"], W["conv_b_pw2"], j, min(CONV_GROUP, b), s)
                conv_new.append(jnp.concatenate([prev, u3], axis=1)[:, -(cw - 1):])
            h = hn.reshape(t, d)
        else:
            w = W["mla"][j]
            if is_prompt:
                lat, kpe, q, k, v = _mla_pre_prompt(h, gains, g0 + 2, w, tab, seq_tile)
                o = _flash_prompt(q, k, v, b, s, seq_tile)
                h = _mla_post_prompt(o, h, gains, g0 + 3, w["wo"])
            else:
                lat, kpe, qlat, qpe = _mla_pre_sample(h, gains, g0 + 2, w, tab, tabq, tm)
                knew = kpe[:, QK_NOPE:QK_NOPE + QK_ROPE]
                lat_pool, kr_pool = mla_pasts[j]
                olat = _decode_sample(page_table, qlat, qpe, lat, knew, lat_pool, jnp.swapaxes(kr_pool, 1, 2))
                h = _mla_post_sample(olat, h, gains, g0 + 3, w["wuv"], w["wo_h"], min(b, 64))
            lat_new.append(lat.reshape(b, s, KV_LORA))
            rope_new.append(kpe[:, QK_NOPE:QK_NOPE + QK_ROPE].reshape(b, s, QK_ROPE))
        h = _xattn(h.reshape(b, s, d), mem_k, mem_v, i * mem_rows_per_layer, gains, g0 + 4, g0 + 5,
                   W["xa_w_q"], W["xa_w_o"], i, xa_group, xa_ts).reshape(t, d)
        h = _ffn(h, gains, g0 + 6, g0 + 7, W["ffn_w_in"], W["ffn_w_out"], 2 * i + 1)
    return h.reshape(b, s, d), conv_new, lat_new, rope_new


def kernel(x_prompt, x_sample, state_conv_l0, state_conv_l2, cache_mla_latent_l1, cache_mla_krope_l1, cache_mla_latent_l3, cache_mla_krope_l3, cache_mem_k, cache_mem_v, page_table, mem_prompt, norm_gain, ffn_w_in, ffn_w_out, conv_w_pw1, conv_b_pw1, conv_w_dw, conv_b_dw, conv_ln_g, conv_ln_b, conv_w_pw2, conv_b_pw2, mla_w_in, mla_q_norm, mla_kv_norm, mla_w_uq, mla_w_ukv, mla_w_o, xa_mem_norm, xa_w_q, xa_w_kv, xa_w_o):
    depth, _, d = norm_gain.shape
    dff = ffn_w_out.shape[2]
    W = dict(
        gains=norm_gain.reshape(depth * 8, 1, d),
        ffn_w_in=ffn_w_in.reshape(depth * 2, d, 2 * dff).astype(BF16),
        ffn_w_out=ffn_w_out.reshape(depth * 2, dff, d).astype(BF16),
        conv_w_pw1=conv_w_pw1.astype(BF16), conv_b_pw1=conv_b_pw1[:, None, :],
        conv_w_dw=conv_w_dw, conv_b_dw=conv_b_dw[:, None, :],
        conv_ln_g=conv_ln_g[:, None, :], conv_ln_b=conv_ln_b[:, None, :],
        conv_w_pw2=conv_w_pw2.astype(BF16), conv_b_pw2=conv_b_pw2[:, None, :],
        xa_w_q=xa_w_q.astype(BF16), xa_w_o=xa_w_o.astype(BF16),
        mla=[_mla_weights(mla_w_in[j], mla_q_norm[j], mla_kv_norm[j], mla_w_uq[j], mla_w_ukv[j], mla_w_o[j])
             for j in range(mla_w_in.shape[0])],
    )

    bp, n_mem, _ = mem_prompt.shape
    mk, mv = _memkv(mem_prompt.reshape(bp * n_mem, d), xa_mem_norm[:, None, :], xa_w_kv.astype(BF16))
    xw = mk.shape[-1]
    y_prompt, p_conv, p_lat, p_rope = _trunk(
        x_prompt, True, W, None, None, None,
        mk.reshape(depth * bp, n_mem, xw), mv.reshape(depth * bp, n_mem, xw), bp)
    xa_heads = xw // XA_HEAD_DIM
    mem_k_prompt = mk.reshape(depth, bp, n_mem, xa_heads, XA_HEAD_DIM)
    mem_v_prompt = mv.reshape(depth, bp, n_mem, xa_heads, XA_HEAD_DIM)

    bs = x_sample.shape[0]
    y_sample, s_conv, s_lat, s_rope = _trunk(
        x_sample, False, W, [state_conv_l0, state_conv_l2],
        [(cache_mla_latent_l1, cache_mla_krope_l1), (cache_mla_latent_l3, cache_mla_krope_l3)], page_table,
        cache_mem_k.reshape(depth * bs, n_mem * xa_heads, XA_HEAD_DIM),
        cache_mem_v.reshape(depth * bs, n_mem * xa_heads, XA_HEAD_DIM), bs)

    return (y_prompt, y_sample,
            p_conv[0], p_conv[1], p_lat[0], p_rope[0], p_lat[1], p_rope[1],
            mem_k_prompt, mem_v_prompt,
            s_conv[0], s_conv[1], s_lat[0], s_rope[0], s_lat[1], s_rope[1])
```
